```python
import jax, jax.numpy as jnp
from jax import lax
import numpy as np

D_MODEL = 1024
BATCH = 4
SEQ = 4096
DEPTH = 2

D_A = 2 * D_MODEL
CHUNK = 128
G_A = D_A // 128
N_HEADS_B = 8
HEAD_DIM_B = 128
D_B = N_HEADS_B * HEAD_DIM_B
PATTERNS = ((128, 1), (512, 4), (2048, 16))
N_SIDE = 64
BLK = 64
NEG = -1e30
EPS = 1e-6

SPLIT_WIDTHS = (D_A, D_A, D_A, D_B, D_B, D_B, D_B, D_MODEL, D_MODEL)
N_IN = sum(SPLIT_WIDTHS)
SPLIT_OFFS = tuple(int(o) for o in np.cumsum(SPLIT_WIDTHS)[:-1])

kernel_name = "hybrid_gmlp_dilated_attn_encoder"


def rms_norm(x, g):
    x32 = x.astype(jnp.float32)
    y = x32 * lax.rsqrt(jnp.mean(x32 * x32, axis=-1, keepdims=True) + EPS)
    return (y * g.astype(jnp.float32)).astype(x.dtype)


def layer_norm(x, g, b):
    x32 = x.astype(jnp.float32)
    mu = jnp.mean(x32, axis=-1, keepdims=True)
    var = jnp.mean(jnp.square(x32 - mu), axis=-1, keepdims=True)
    y = (x32 - mu) * lax.rsqrt(var + EPS)
    return (y * g.astype(jnp.float32) + b.astype(jnp.float32)).astype(x.dtype)


def alibi_slopes(n):
    return jnp.exp2(-8.0 * (jnp.arange(n, dtype=jnp.float32) + 1.0) / n)


def spatial_gating(u, v, ln_g, ln_b, w_s, b_s):
    B, S, C = v.shape
    v = layer_norm(v, ln_g, ln_b)
    vc = v.reshape(B, S // CHUNK, CHUNK, G_A, C // G_A)
    s = jnp.einsum('gpq,bcqge->bcpge', w_s, vc) + b_s.T[:, :, None]
    return u * s.reshape(B, S, C)


def dilated_window_attention(q, k, v, dilation, slopes):
    B, S, H, E = q.shape
    L = S // dilation
    nb = -(-L // BLK)
    Lp = nb * BLK

    def to_sub(t):
        return t.reshape(B, L, dilation, H, E).transpose(0, 2, 3, 1, 4)

    qs = jnp.pad(to_sub(q), [(0, 0)] * 3 + [(0, Lp - L), (0, 0)]).reshape(B, dilation, H, nb, BLK, E)

    def key_blocks(t):
        tp = jnp.pad(to_sub(t), [(0, 0)] * 3 + [(BLK, Lp - L + BLK), (0, 0)])
        tp = tp.reshape(B, dilation, H, nb + 2, BLK, E)
        return jnp.concatenate([tp[:, :, :, :-2], tp[:, :, :, 1:-1], tp[:, :, :, 2:]], axis=4)

    kb = key_blocks(k)
    vb = key_blocks(v)
    qa = jnp.arange(BLK)
    kc = jnp.arange(3 * BLK)
    rel = kc[None, :] - BLK - qa[:, None]
    j_idx = (jnp.arange(nb)[:, None] - 1) * BLK + kc[None, :]
    valid = (jnp.abs(rel)[None] <= N_SIDE) & ((j_idx >= 0) & (j_idx < L))[:, None, :]
    dist = (dilation * jnp.abs(rel)).astype(jnp.float32)

    s = jnp.einsum('bdhnqe,bdhnke->bdhnqk', qs, kb).astype(jnp.float32) * (E ** -0.5)
    s = s - slopes[:, None, None, None] * dist
    s = jnp.where(valid, s, NEG)
    lse = jax.nn.logsumexp(s, axis=-1)
    p = jnp.exp(s - lse[..., None]).astype(v.dtype)
    o = jnp.einsum('bdhnqk,bdhnke->bdhnqe', p, vb)
    o = o.reshape(B, dilation, H, Lp, E)[:, :, :, :L].transpose(0, 3, 1, 2, 4).reshape(B, S, H, E)
    lse = lse.reshape(B, dilation, H, Lp)[..., :L].transpose(0, 3, 1, 2).reshape(B, S, H)
    return o, lse


def mixture_of_dilations(q, k, v):
    slopes = alibi_slopes(N_HEADS_B)
    outs, lses = [], []
    for _, dilation in PATTERNS:
        o, l = dilated_window_attention(q, k, v, dilation, slopes)
        outs.append(o)
        lses.append(l)
    w = jax.nn.softmax(jnp.stack(lses, axis=0), axis=0)
    o = jnp.stack(outs, axis=0)
    return jnp.sum(w[..., None].astype(o.dtype) * o, axis=0)


def hybrid_layer(x, w_in, b_gate, g_pre, g_post, ln_g, ln_b, w_s, b_s, w_pa, w_pb, w_o):
    B, S, _ = x.shape
    h = rms_norm(x, g_pre)
    proj = h @ w_in
    u_a, v_a, z_a, q_b, k_b, v_b, z_b, gate_a, gate_b = jnp.split(proj, SPLIT_OFFS, axis=-1)
    y_a = spatial_gating(jax.nn.gelu(u_a), jax.nn.gelu(v_a), ln_g, ln_b, w_s, b_s) * jax.nn.silu(z_a)
    hs = (B, S, N_HEADS_B, HEAD_DIM_B)
    y_b = mixture_of_dilations(q_b.reshape(hs), k_b.reshape(hs), v_b.reshape(hs)).reshape(B, S, D_B)
    y_b = y_b * jax.nn.silu(z_b)
    g_a = jax.nn.sigmoid(gate_a + b_gate[:D_MODEL])
    g_b = jax.nn.sigmoid(gate_b + b_gate[D_MODEL:])
    m = g_a * (y_a @ w_pa) + g_b * (y_b @ w_pb)
    return x + rms_norm(m @ w_o, g_post)


def setup_inputs(seed: int = 0) -> dict:
    key = jax.random.key(seed)
    ks = jax.random.split(key, 14)
    f32 = jnp.float32
    nrm = lambda k, shape, scale: jax.random.normal(k, shape, f32) * scale
    return {
        "x": jax.random.normal(ks[0], (BATCH, SEQ, D_MODEL), f32),
        "w_in": nrm(ks[1], (DEPTH, D_MODEL, N_IN), D_MODEL ** -0.5),
        "b_gate": nrm(ks[2], (DEPTH, 2 * D_MODEL), 0.02),
        "g_pre": 1.0 + nrm(ks[3], (DEPTH, D_MODEL), 0.02),
        "g_post": 1.0 + nrm(ks[4], (DEPTH, D_MODEL), 0.02),
        "sgu_ln_g": 1.0 + nrm(ks[5], (DEPTH, D_A), 0.02),
        "sgu_ln_b": nrm(ks[6], (DEPTH, D_A), 0.02),
        "w_spatial": nrm(ks[7], (DEPTH, G_A, CHUNK, CHUNK), CHUNK ** -0.5),
        "b_spatial": 1.0 + nrm(ks[8], (DEPTH, G_A, CHUNK), 0.1),
        "w_proj_a": nrm(ks[9], (DEPTH, D_A, D_MODEL), D_A ** -0.5),
        "w_proj_b": nrm(ks[10], (DEPTH, D_B, D_MODEL), D_B ** -0.5),
        "w_out": nrm(ks[11], (DEPTH, D_MODEL, D_MODEL), D_MODEL ** -0.5),
    }


def reference(x, w_in, b_gate, g_pre, g_post, sgu_ln_g, sgu_ln_b, w_spatial, b_spatial,
              w_proj_a, w_proj_b, w_out):
    for l in range(DEPTH):
        x = hybrid_layer(x, w_in[l], b_gate[l], g_pre[l], g_post[l], sgu_ln_g[l], sgu_ln_b[l],
                         w_spatial[l], b_spatial[l], w_proj_a[l], w_proj_b[l], w_out[l])
    return x
```

```python
import functools
import math

import jax
import jax.numpy as jnp
from jax import lax
from jax.experimental import pallas as pl
from jax.experimental.pallas import tpu as pltpu

D_MODEL = 1024
D_A = 2 * D_MODEL
CHUNK = 128
G_A = D_A // 128
N_HEADS = 8
HEAD_DIM = 128
D_B = N_HEADS * HEAD_DIM
DILATIONS = (1, 4, 16)
N_SIDE = 64
NEG = -1e30
EPS = 1e-6

OFF_U = 0
OFF_V = OFF_U + D_A
OFF_ZA = OFF_V + D_A
OFF_Q = OFF_ZA + D_A
OFF_K = OFF_Q + D_B
OFF_VB = OFF_K + D_B
OFF_ZB = OFF_VB + D_B
OFF_GA = OFF_ZB + D_B
OFF_GB = OFF_GA + D_MODEL
N_IN = OFF_GB + D_MODEL

QBLK = 64
KBLK = QBLK + 2 * N_SIDE
TM_PROJ = 256
TM_OUT = 512
CW = 512
MIB = 1024 * 1024

F32 = jnp.float32
BF16 = jnp.bfloat16


def _gelu(x):
    c = math.sqrt(2.0 / math.pi)
    return 0.5 * x * (1.0 + jnp.tanh(c * (x + 0.044715 * (x * x * x))))


def _sigmoid(x):
    return 0.5 * (jnp.tanh(0.5 * x) + 1.0)


def _silu(x):
    return x * _sigmoid(x)


def _const_spec(shape):
    nd = len(shape)
    return pl.BlockSpec(shape, lambda *_: (0,) * nd, pipeline_mode=pl.Buffered(1))


def _proj_kernel(x_ref, gpre_ref, w_in_ref, bgate_ref, lng_ref, lnb_ref, ws_ref, bst_ref, wpa_ref,
                 q_ref, k_ref, v_ref, zb_ref, gb_ref, ma_ref,
                 t_scr, v_scr, vn_scr, ya_scr):
    tm = x_ref.shape[0]
    x = x_ref[...]
    inv = lax.rsqrt(jnp.mean(x * x, axis=-1, keepdims=True) + EPS)
    h = (x * inv * gpre_ref[...]).astype(BF16)

    def proj(off, c):
        return jnp.dot(h, w_in_ref[:, off + c:off + c + CW], preferred_element_type=F32)

    for c in range(0, D_A, CW):
        t_scr[:, c:c + CW] = _gelu(proj(OFF_U, c)) * _silu(proj(OFF_ZA, c))
        v_scr[:, c:c + CW] = _gelu(proj(OFF_V, c))

    v = v_scr[...]
    mu = jnp.mean(v, axis=-1, keepdims=True)
    dv = v - mu
    var = jnp.mean(dv * dv, axis=-1, keepdims=True)
    vn_scr[...] = (dv * lax.rsqrt(var + EPS) * lng_ref[...] + lnb_ref[...]).astype(BF16)

    for c in range(tm // CHUNK):
        rows = slice(c * CHUNK, (c + 1) * CHUNK)
        for g in range(G_A):
            cols = slice(g * 128, (g + 1) * 128)
            s = jnp.dot(ws_ref[g], vn_scr[rows, cols], preferred_element_type=F32)
            s = s + bst_ref[:, g:g + 1]
            ya_scr[rows, cols] = (t_scr[rows, cols] * s).astype(BF16)

    scale = HEAD_DIM ** -0.5
    for c in range(0, D_MODEL, CW):
        cs = slice(c, c + CW)
        ya_p = jnp.dot(ya_scr[...], wpa_ref[:, cs], preferred_element_type=F32)
        ma_ref[:, cs] = _sigmoid(proj(OFF_GA, c) + bgate_ref[:, cs]) * ya_p
        gb_ref[:, cs] = _sigmoid(proj(OFF_GB, c) + bgate_ref[:, D_MODEL + c:D_MODEL + c + CW]).astype(BF16)
        zb_ref[:, cs] = _silu(proj(OFF_ZB, c)).astype(BF16)
        q_ref[:, cs] = proj(OFF_Q, c) * scale
        k_ref[:, cs] = proj(OFF_K, c)
        v_ref[:, cs] = proj(OFF_VB, c)


def _proj_call(x, gpre, w_in, bgate, lng, lnb, ws, bst, wpa):
    m = x.shape[0]
    tm = TM_PROJ
    row = lambda i: (i, 0)
    out_f32 = jax.ShapeDtypeStruct((m, D_B), F32)
    out_bf16 = jax.ShapeDtypeStruct((m, D_B), BF16)
    return pl.pallas_call(
        _proj_kernel,
        grid=(m // tm,),
        in_specs=[
            pl.BlockSpec((tm, D_MODEL), row),
            _const_spec((1, D_MODEL)),
            _const_spec((D_MODEL, N_IN)),
            _const_spec((1, 2 * D_MODEL)),
            _const_spec((1, D_A)),
            _const_spec((1, D_A)),
            _const_spec((G_A, CHUNK, CHUNK)),
            _const_spec((CHUNK, G_A)),
            _const_spec((D_A, D_MODEL)),
        ],
        out_specs=[pl.BlockSpec((tm, D_B), row)] * 6,
        out_shape=[out_f32, out_f32, out_f32, out_bf16, out_bf16, out_f32],
        scratch_shapes=[
            pltpu.VMEM((tm, D_A), F32),
            pltpu.VMEM((tm, D_A), F32),
            pltpu.VMEM((tm, D_A), BF16),
            pltpu.VMEM((tm, D_A), BF16),
        ],
        compiler_params=pltpu.CompilerParams(
            dimension_semantics=("arbitrary",), vmem_limit_bytes=56 * MIB),
        name="proj_gmlp",
    )(x, gpre, w_in, bgate, lng, lnb, ws, bst, wpa)


def _attn_kernel(q_ref, k_ref, v_ref, y_ref,
                 qd, kd, vd, o1, o4, o16, l1, l4, l16, bias_scr):
    seq = q_ref.shape[0]
    head = pl.program_id(1)
    o_scr = (o1, o4, o16)
    l_scr = (l1, l4, l16)

    slope = jnp.exp2(jnp.full((QBLK, KBLK), -1.0, F32) * (head + 1).astype(F32))
    qi = lax.broadcasted_iota(jnp.int32, (QBLK, KBLK), 0)
    ki = lax.broadcasted_iota(jnp.int32, (QBLK, KBLK), 1)
    for p, d in enumerate(DILATIONS):
        for case, shift in enumerate((N_SIDE, 0, 2 * N_SIDE)):
            rel = jnp.abs(ki - shift - qi)
            bias_scr[3 * p + case] = jnp.where(rel <= N_SIDE, -slope * (d * rel).astype(F32), NEG)

    for p, d in enumerate(DILATIONS):
        sub_len = seq // d
        nb = sub_len // QBLK
        rows_per_copy = 256
        copies_per_res = sub_len // rows_per_copy

        def deinterleave(i, carry, d=d, copies_per_res=copies_per_res):
            r = i // copies_per_res
            c = i % copies_per_res
            dst = pl.ds(pl.multiple_of(i * rows_per_copy, rows_per_copy), rows_per_copy)
            if d == 1:
                src = dst
            else:
                src = pl.ds(r + d * rows_per_copy * c, rows_per_copy, stride=d)
            qd[dst, :] = q_ref[src, :].astype(BF16)
            kd[dst, :] = k_ref[src, :].astype(BF16)
            vd[dst, :] = v_ref[src, :].astype(BF16)
            return carry

        lax.fori_loop(0, seq // rows_per_copy, deinterleave, 0)

        def block(i, carry, p=p, sub_len=sub_len, nb=nb):
            r = i // nb
            n = i % nb
            case = jnp.where(n == 0, 1, jnp.where(n == nb - 1, 2, 0))
            k0 = r * sub_len + jnp.clip(QBLK * (n - 1), 0, sub_len - KBLK)
            qs = pl.ds(pl.multiple_of(i * QBLK, QBLK), QBLK)
            ks = pl.ds(pl.multiple_of(k0, QBLK), KBLK)
            s = lax.dot_general(qd[qs, :], kd[ks, :], (((1,), (1,)), ((), ())),
                                preferred_element_type=F32)
            s = s + bias_scr[3 * p + case]
            mx = jnp.max(s, axis=-1, keepdims=True)
            e = jnp.exp(s - mx)
            den = jnp.sum(e, axis=-1, keepdims=True)
            o = jnp.dot(e.astype(BF16), vd[ks, :], preferred_element_type=F32)
            o_scr[p][qs, :] = o * (1.0 / den)
            l_scr[p][qs, :] = jnp.broadcast_to(mx + jnp.log(den), (QBLK, HEAD_DIM))
            return carry

        lax.fori_loop(0, seq // QBLK, block, 0, unroll=4)

    nb16 = seq // 16 // QBLK

    def combine(i, carry):
        r16 = i // nb16
        n16 = i % nb16
        s16 = pl.ds(pl.multiple_of(i * QBLK, QBLK), QBLK)
        s4 = pl.ds((r16 % 4) * (seq // 4) + r16 // 4 + 4 * QBLK * n16, QBLK, stride=4)
        s1 = pl.ds(r16 + 16 * QBLK * n16, QBLK, stride=16)
        a1, a4, a16 = l1[s1, :], l4[s4, :], l16[s16, :]
        mx = jnp.maximum(jnp.maximum(a1, a4), a16)
        e1, e4, e16 = jnp.exp(a1 - mx), jnp.exp(a4 - mx), jnp.exp(a16 - mx)
        num = e1 * o1[s1, :] + e4 * o4[s4, :] + e16 * o16[s16, :]
        o1[s1, :] = num / (e1 + e4 + e16)
        return carry

    lax.fori_loop(0, seq // QBLK, combine, 0)
    y_ref[...] = o1[...].astype(BF16)


def _attn_call(q, k, v, batch, seq):
    blk = pl.BlockSpec((seq, HEAD_DIM), lambda b, h: (b, h))
    seq_f32 = pltpu.VMEM((seq, HEAD_DIM), F32)
    seq_bf16 = pltpu.VMEM((seq, HEAD_DIM), BF16)
    return pl.pallas_call(
        _attn_kernel,
        grid=(batch, N_HEADS),
        in_specs=[blk, blk, blk],
        out_specs=blk,
        out_shape=jax.ShapeDtypeStruct((batch * seq, D_B), BF16),
        scratch_shapes=[seq_bf16] * 3 + [seq_f32] * 6
        + [pltpu.VMEM((3 * len(DILATIONS), QBLK, KBLK), F32)],
        compiler_params=pltpu.CompilerParams(
            dimension_semantics=("arbitrary", "arbitrary"), vmem_limit_bytes=48 * MIB),
        name="dilated_attn",
    )(q, k, v)


def _out_kernel(yb_ref, zb_ref, gb_ref, ma_ref, x_ref, wpb_ref, wo_ref, gpost_ref, o_ref):
    yb = (yb_ref[...].astype(F32) * zb_ref[...].astype(F32)).astype(BF16)
    mb = jnp.dot(yb, wpb_ref[...], preferred_element_type=F32)
    m = ma_ref[...] + gb_ref[...].astype(F32) * mb
    r = jnp.dot(m.astype(BF16), wo_ref[...], preferred_element_type=F32)
    inv = lax.rsqrt(jnp.mean(r * r, axis=-1, keepdims=True) + EPS)
    o_ref[...] = x_ref[...] + r * inv * gpost_ref[...]


def _out_call(yb, zb, gb, ma, x, wpb, wo, gpost):
    m = x.shape[0]
    tm = TM_OUT
    row = pl.BlockSpec((tm, D_MODEL), lambda i: (i, 0))
    return pl.pallas_call(
        _out_kernel,
        grid=(m // tm,),
        in_specs=[row, row, row, row, row,
                  _const_spec((D_B, D_MODEL)), _const_spec((D_MODEL, D_MODEL)),
                  _const_spec((1, D_MODEL))],
        out_specs=row,
        out_shape=jax.ShapeDtypeStruct((m, D_MODEL), F32),
        compiler_params=pltpu.CompilerParams(
            dimension_semantics=("arbitrary",), vmem_limit_bytes=40 * MIB),
        name="merge_out",
    )(yb, zb, gb, ma, x, wpb, wo, gpost)


def kernel(x, w_in, b_gate, g_pre, g_post, sgu_ln_g, sgu_ln_b, w_spatial, b_spatial,
           w_proj_a, w_proj_b, w_out):
    batch, seq, d_model = x.shape
    depth = w_in.shape[0]
    assert d_model == D_MODEL and w_in.shape[2] == N_IN
    assert seq % (16 * 256) == 0 and (batch * seq) % TM_OUT == 0
    xf = x.reshape(batch * seq, d_model)
    for l in range(depth):
        q, k, v, zb, gb, ma = _proj_call(
            xf, g_pre[l][None], w_in[l].astype(BF16), b_gate[l][None],
            sgu_ln_g[l][None], sgu_ln_b[l][None], w_spatial[l].astype(BF16),
            b_spatial[l].T, w_proj_a[l].astype(BF16))
        yb = _attn_call(q, k, v, batch, seq)
        xf = _out_call(yb, zb, gb, ma, xf, w_proj_b[l].astype(BF16), w_out[l].astype(BF16),
                       g_post[l][None])
    return xf.reshape(batch, seq, d_model)
```

```python
import functools
import math

import jax
import jax.numpy as jnp
from jax import lax
from jax.experimental import pallas as pl
from jax.experimental.pallas import tpu as pltpu

D_MODEL = 1024
D_A = 2 * D_MODEL
CHUNK = 128
G_A = D_A // 128
N_HEADS = 8
HEAD_DIM = 128
D_B = N_HEADS * HEAD_DIM
DILATIONS = (1, 4, 16)
N_SIDE = 64
NEG = -1e30
EPS = 1e-6

OFF_U = 0
OFF_V = OFF_U + D_A
OFF_ZA = OFF_V + D_A
OFF_Q = OFF_ZA + D_A
OFF_K = OFF_Q + D_B
OFF_VB = OFF_K + D_B
OFF_ZB = OFF_VB + D_B
OFF_GA = OFF_ZB + D_B
OFF_GB = OFF_GA + D_MODEL
N_IN = OFF_GB + D_MODEL

QBLK = 128
KBLK = QBLK + 2 * N_SIDE
GROUP = 4
TM_PROJ = 256
TM_OUT = 512
CW = 512
MIB = 1024 * 1024

F32 = jnp.float32
BF16 = jnp.bfloat16


def _gelu(x):
    c = math.sqrt(2.0 / math.pi)
    return 0.5 * x * (1.0 + jnp.tanh(c * (x + 0.044715 * (x * x * x))))


def _sigmoid(x):
    return 0.5 * (jnp.tanh(0.5 * x) + 1.0)


def _silu(x):
    return x * _sigmoid(x)


def _const_spec(shape):
    nd = len(shape)
    return pl.BlockSpec(shape, lambda *_: (0,) * nd, pipeline_mode=pl.Buffered(1))


def _proj_kernel(x_ref, gpre_ref, w_in_ref, bgate_ref, lng_ref, lnb_ref, ws_ref, bst_ref, wpa_ref,
                 q_ref, k_ref, v_ref, zb_ref, gb_ref, ma_ref,
                 t_scr, v_scr, vn_scr, ya_scr):
    tm = x_ref.shape[0]
    x = x_ref[...]
    inv = lax.rsqrt(jnp.mean(x * x, axis=-1, keepdims=True) + EPS)
    h = (x * inv * gpre_ref[...]).astype(BF16)

    def proj(off, c):
        return jnp.dot(h, w_in_ref[:, off + c:off + c + CW], preferred_element_type=F32)

    for c in range(0, D_A, CW):
        t_scr[:, c:c + CW] = _gelu(proj(OFF_U, c)) * _silu(proj(OFF_ZA, c))
        v_scr[:, c:c + CW] = _gelu(proj(OFF_V, c))

    v = v_scr[...]
    mu = jnp.mean(v, axis=-1, keepdims=True)
    dv = v - mu
    var = jnp.mean(dv * dv, axis=-1, keepdims=True)
    vn_scr[...] = (dv * lax.rsqrt(var + EPS) * lng_ref[...] + lnb_ref[...]).astype(BF16)

    for c in range(tm // CHUNK):
        rows = slice(c * CHUNK, (c + 1) * CHUNK)
        for g in range(G_A):
            cols = slice(g * 128, (g + 1) * 128)
            s = jnp.dot(ws_ref[g], vn_scr[rows, cols], preferred_element_type=F32)
            s = s + bst_ref[:, g:g + 1]
            ya_scr[rows, cols] = (t_scr[rows, cols] * s).astype(BF16)

    scale = HEAD_DIM ** -0.5
    for c in range(0, D_MODEL, CW):
        cs = slice(c, c + CW)
        ya_p = jnp.dot(ya_scr[...], wpa_ref[:, cs], preferred_element_type=F32)
        ma_ref[:, cs] = _sigmoid(proj(OFF_GA, c) + bgate_ref[:, cs]) * ya_p
        gb_ref[:, cs] = _sigmoid(proj(OFF_GB, c) + bgate_ref[:, D_MODEL + c:D_MODEL + c + CW]).astype(BF16)
        zb_ref[:, cs] = _silu(proj(OFF_ZB, c)).astype(BF16)
        q_ref[:, cs] = proj(OFF_Q, c) * scale
        k_ref[:, cs] = proj(OFF_K, c)
        v_ref[:, cs] = proj(OFF_VB, c)


def _proj_call(x, gpre, w_in, bgate, lng, lnb, ws, bst, wpa):
    m = x.shape[0]
    tm = TM_PROJ
    row = lambda i: (i, 0)
    out_f32 = jax.ShapeDtypeStruct((m, D_B), F32)
    out_bf16 = jax.ShapeDtypeStruct((m, D_B), BF16)
    return pl.pallas_call(
        _proj_kernel,
        grid=(m // tm,),
        in_specs=[
            pl.BlockSpec((tm, D_MODEL), row),
            _const_spec((1, D_MODEL)),
            _const_spec((D_MODEL, N_IN)),
            _const_spec((1, 2 * D_MODEL)),
            _const_spec((1, D_A)),
            _const_spec((1, D_A)),
            _const_spec((G_A, CHUNK, CHUNK)),
            _const_spec((CHUNK, G_A)),
            _const_spec((D_A, D_MODEL)),
        ],
        out_specs=[pl.BlockSpec((tm, D_B), row)] * 6,
        out_shape=[out_f32, out_f32, out_f32, out_bf16, out_bf16, out_f32],
        scratch_shapes=[
            pltpu.VMEM((tm, D_A), F32),
            pltpu.VMEM((tm, D_A), F32),
            pltpu.VMEM((tm, D_A), BF16),
            pltpu.VMEM((tm, D_A), BF16),
        ],
        compiler_params=pltpu.CompilerParams(
            dimension_semantics=("arbitrary",), vmem_limit_bytes=56 * MIB),
        name="proj_gmlp",
    )(x, gpre, w_in, bgate, lng, lnb, ws, bst, wpa)


def _attn_kernel(q_ref, k_ref, v_ref, y_ref, qd, kd, vd, o_scr, l_scr, p_scr, bias_scr):
    seq = q_ref.shape[0]
    n_pat = len(DILATIONS)
    blocks_per_pat = seq // QBLK
    log2_blocks = blocks_per_pat.bit_length() - 1
    head = pl.program_id(1)

    slope = jnp.exp2(jnp.full((QBLK, KBLK), -1.0, F32) * (head + 1).astype(F32))
    qi = lax.broadcasted_iota(jnp.int32, (QBLK, KBLK), 0)
    ki = lax.broadcasted_iota(jnp.int32, (QBLK, KBLK), 1)
    for p, d in enumerate(DILATIONS):
        for case, shift in enumerate((N_SIDE, 0, KBLK - QBLK)):
            rel = jnp.abs(ki - shift - qi)
            bias_scr[3 * p + case] = jnp.where(rel <= N_SIDE, -slope * (d * rel).astype(F32), NEG)

    rows_per_copy = 256
    for p, d in enumerate(DILATIONS):
        copies_per_res = seq // d // rows_per_copy

        def deinterleave(i, carry, p=p, d=d, copies_per_res=copies_per_res):
            r = i // copies_per_res
            c = i % copies_per_res
            dst = pl.ds(pl.multiple_of(i * rows_per_copy, rows_per_copy), rows_per_copy)
            if d == 1:
                src = dst
            else:
                src = pl.ds(r + d * rows_per_copy * c, rows_per_copy, stride=d)
            qd[p, dst, :] = q_ref[src, :].astype(BF16)
            kd[p, dst, :] = k_ref[src, :].astype(BF16)
            vd[p, dst, :] = v_ref[src, :].astype(BF16)
            return carry

        lax.fori_loop(0, seq // rows_per_copy, deinterleave, 0)

    def block_addr(g):
        p = lax.shift_right_logical(g, jnp.int32(log2_blocks))
        j = g & (blocks_per_pat - 1)
        log2_per_res = log2_blocks - 2 * p
        per_res = lax.shift_left(jnp.int32(1), log2_per_res)
        r = lax.shift_right_logical(j, log2_per_res)
        n = j & (per_res - 1)
        sub_len = lax.shift_right_logical(jnp.int32(seq), 2 * p)
        case = jnp.where(n == 0, 1, jnp.where(n == per_res - 1, 2, 0))
        k0 = r * sub_len + jnp.clip(QBLK * n - N_SIDE, 0, sub_len - KBLK)
        return p, pl.multiple_of(j * QBLK, QBLK), pl.multiple_of(k0, N_SIDE), 3 * p + case

    def scores_stage(grp, slot):
        for b in range(GROUP):
            p, q0, k0, bias_idx = block_addr(grp * GROUP + b)
            s = lax.dot_general(qd[p, pl.ds(q0, QBLK), :], kd[p, pl.ds(k0, KBLK), :],
                                (((1,), (1,)), ((), ())), preferred_element_type=F32)
            s = s + bias_scr[bias_idx]
            mx = jnp.max(s, axis=-1, keepdims=True)
            e = jnp.exp(s - mx)
            den = jnp.sum(e, axis=-1, keepdims=True)
            p_scr[slot, b * QBLK:(b + 1) * QBLK, :] = (e * (1.0 / den)).astype(BF16)
            l_scr[p, pl.ds(q0, QBLK), :] = jnp.broadcast_to(mx + jnp.log(den), (QBLK, HEAD_DIM))

    def values_stage(grp, slot):
        for b in range(GROUP):
            p, q0, k0, _ = block_addr(grp * GROUP + b)
            o_scr[p, pl.ds(q0, QBLK), :] = jnp.dot(
                p_scr[slot, b * QBLK:(b + 1) * QBLK, :], vd[p, pl.ds(k0, KBLK), :],
                preferred_element_type=F32)

    n_groups = n_pat * blocks_per_pat // GROUP
    scores_stage(jnp.int32(0), 0)

    def pipelined(i, carry):
        values_stage(i - 1, (i - 1) & 1)
        scores_stage(i, i & 1)
        return carry

    lax.fori_loop(1, n_groups, pipelined, 0)
    values_stage(jnp.int32(n_groups - 1), (n_groups - 1) & 1)

    rows = 64
    chunks_per_res16 = seq // 16 // rows

    def combine(i, carry):
        r16 = i // chunks_per_res16
        n16 = i % chunks_per_res16
        s16 = pl.ds(pl.multiple_of(i * rows, rows), rows)
        s4 = pl.ds((r16 % 4) * (seq // 4) + r16 // 4 + 4 * rows * n16, rows, stride=4)
        s1 = pl.ds(r16 + 16 * rows * n16, rows, stride=16)
        a1, a4, a16 = l_scr[0, s1, :], l_scr[1, s4, :], l_scr[2, s16, :]
        mx = jnp.maximum(jnp.maximum(a1, a4), a16)
        e1, e4, e16 = jnp.exp(a1 - mx), jnp.exp(a4 - mx), jnp.exp(a16 - mx)
        num = e1 * o_scr[0, s1, :] + e4 * o_scr[1, s4, :] + e16 * o_scr[2, s16, :]
        o_scr[0, s1, :] = num / (e1 + e4 + e16)
        return carry

    lax.fori_loop(0, seq // rows, combine, 0)
    y_ref[...] = o_scr[0].astype(BF16)


def _attn_call(q, k, v, batch, seq):
    n_pat = len(DILATIONS)
    blk = pl.BlockSpec((seq, HEAD_DIM), lambda b, h: (b, h))
    pat_f32 = pltpu.VMEM((n_pat, seq, HEAD_DIM), F32)
    pat_bf16 = pltpu.VMEM((n_pat, seq, HEAD_DIM), BF16)
    return pl.pallas_call(
        _attn_kernel,
        grid=(batch, N_HEADS),
        in_specs=[blk, blk, blk],
        out_specs=blk,
        out_shape=jax.ShapeDtypeStruct((batch * seq, D_B), BF16),
        scratch_shapes=[pat_bf16] * 3 + [pat_f32] * 2
        + [pltpu.VMEM((2, GROUP * QBLK, KBLK), BF16),
           pltpu.VMEM((3 * n_pat, QBLK, KBLK), F32)],
        compiler_params=pltpu.CompilerParams(
            dimension_semantics=("arbitrary", "arbitrary"), vmem_limit_bytes=48 * MIB),
        name="dilated_attn",
    )(q, k, v)


def _out_kernel(yb_ref, zb_ref, gb_ref, ma_ref, x_ref, wpb_ref, wo_ref, gpost_ref, o_ref):
    yb = (yb_ref[...].astype(F32) * zb_ref[...].astype(F32)).astype(BF16)
    mb = jnp.dot(yb, wpb_ref[...], preferred_element_type=F32)
    m = ma_ref[...] + gb_ref[...].astype(F32) * mb
    r = jnp.dot(m.astype(BF16), wo_ref[...], preferred_element_type=F32)
    inv = lax.rsqrt(jnp.mean(r * r, axis=-1, keepdims=True) + EPS)
    o_ref[...] = x_ref[...] + r * inv * gpost_ref[...]


def _out_call(yb, zb, gb, ma, x, wpb, wo, gpost):
    m = x.shape[0]
    tm = TM_OUT
    row = pl.BlockSpec((tm, D_MODEL), lambda i: (i, 0))
    return pl.pallas_call(
        _out_kernel,
        grid=(m // tm,),
        in_specs=[row, row, row, row, row,
                  _const_spec((D_B, D_MODEL)), _const_spec((D_MODEL, D_MODEL)),
                  _const_spec((1, D_MODEL))],
        out_specs=row,
        out_shape=jax.ShapeDtypeStruct((m, D_MODEL), F32),
        compiler_params=pltpu.CompilerParams(
            dimension_semantics=("arbitrary",), vmem_limit_bytes=40 * MIB),
        name="merge_out",
    )(yb, zb, gb, ma, x, wpb, wo, gpost)


def kernel(x, w_in, b_gate, g_pre, g_post, sgu_ln_g, sgu_ln_b, w_spatial, b_spatial,
           w_proj_a, w_proj_b, w_out):
    batch, seq, d_model = x.shape
    depth = w_in.shape[0]
    assert d_model == D_MODEL and w_in.shape[2] == N_IN
    assert seq % (16 * 256) == 0 and (batch * seq) % TM_OUT == 0
    xf = x.reshape(batch * seq, d_model)
    for l in range(depth):
        q, k, v, zb, gb, ma = _proj_call(
            xf, g_pre[l][None], w_in[l].astype(BF16), b_gate[l][None],
            sgu_ln_g[l][None], sgu_ln_b[l][None], w_spatial[l].astype(BF16),
            b_spatial[l].T, w_proj_a[l].astype(BF16))
        yb = _attn_call(q, k, v, batch, seq)
        xf = _out_call(yb, zb, gb, ma, xf, w_proj_b[l].astype(BF16), w_out[l].astype(BF16),
                       g_post[l][None])
    return xf.reshape(batch, seq, d_model)
```

```python
import functools
import math

import jax
import jax.numpy as jnp
from jax import lax
from jax.experimental import pallas as pl
from jax.experimental.pallas import tpu as pltpu

D_MODEL = 1024
D_A = 2 * D_MODEL
CHUNK = 128
G_A = D_A // 128
N_HEADS = 8
HEAD_DIM = 128
D_B = N_HEADS * HEAD_DIM
DILATIONS = (1, 4, 16)
N_SIDE = 64
NEG = -1e30
EPS = 1e-6
LOG2E = math.log2(math.e)

OFF_U = 0
OFF_V = OFF_U + D_A
OFF_ZA = OFF_V + D_A
OFF_Q = OFF_ZA + D_A
OFF_K = OFF_Q + D_B
OFF_VB = OFF_K + D_B
OFF_ZB = OFF_VB + D_B
OFF_GA = OFF_ZB + D_B
OFF_GB = OFF_GA + D_MODEL
N_IN = OFF_GB + D_MODEL

QBLK = 128
KBLK = QBLK + 2 * N_SIDE
GROUP = 8
TM_PROJ = 256
TM_OUT = 512
CW = 512
MIB = 1024 * 1024

F32 = jnp.float32
BF16 = jnp.bfloat16


def _gelu(x):
    c = math.sqrt(2.0 / math.pi)
    return 0.5 * x * (1.0 + jnp.tanh(c * (x + 0.044715 * (x * x * x))))


def _sigmoid(x):
    return 0.5 * (jnp.tanh(0.5 * x) + 1.0)


def _silu(x):
    return x * _sigmoid(x)


def _const_spec(shape):
    nd = len(shape)
    return pl.BlockSpec(shape, lambda *_: (0,) * nd, pipeline_mode=pl.Buffered(1))


def _proj_kernel(x_ref, gpre_ref, w_in_ref, bgate_ref, lng_ref, lnb_ref, ws_ref, bst_ref, wpa_ref,
                 q_ref, k_ref, v_ref, zb_ref, gb_ref, ma_ref,
                 t_scr, v_scr, vn_scr, ya_scr):
    tm = x_ref.shape[0]
    x = x_ref[...]
    inv = lax.rsqrt(jnp.mean(x * x, axis=-1, keepdims=True) + EPS)
    h = (x * inv * gpre_ref[...]).astype(BF16)

    def proj(off, c):
        return jnp.dot(h, w_in_ref[:, off + c:off + c + CW], preferred_element_type=F32)

    for c in range(0, D_A, CW):
        t_scr[:, c:c + CW] = _gelu(proj(OFF_U, c)) * _silu(proj(OFF_ZA, c))
        v_scr[:, c:c + CW] = _gelu(proj(OFF_V, c))

    v = v_scr[...]
    mu = jnp.mean(v, axis=-1, keepdims=True)
    dv = v - mu
    var = jnp.mean(dv * dv, axis=-1, keepdims=True)
    vn_scr[...] = (dv * lax.rsqrt(var + EPS) * lng_ref[...] + lnb_ref[...]).astype(BF16)

    for c in range(tm // CHUNK):
        rows = slice(c * CHUNK, (c + 1) * CHUNK)
        for g in range(G_A):
            cols = slice(g * 128, (g + 1) * 128)
            s = jnp.dot(ws_ref[g], vn_scr[rows, cols], preferred_element_type=F32)
            s = s + bst_ref[:, g:g + 1]
            ya_scr[rows, cols] = (t_scr[rows, cols] * s).astype(BF16)

    scale = LOG2E * HEAD_DIM ** -0.5
    for c in range(0, D_MODEL, CW):
        cs = slice(c, c + CW)
        ya_p = jnp.dot(ya_scr[...], wpa_ref[:, cs], preferred_element_type=F32)
        ma_ref[:, cs] = _sigmoid(proj(OFF_GA, c) + bgate_ref[:, cs]) * ya_p
        gb_ref[:, cs] = _sigmoid(proj(OFF_GB, c) + bgate_ref[:, D_MODEL + c:D_MODEL + c + CW]).astype(BF16)
        zb_ref[:, cs] = _silu(proj(OFF_ZB, c)).astype(BF16)
        q_ref[:, cs] = proj(OFF_Q, c) * scale
        k_ref[:, cs] = proj(OFF_K, c)
        v_ref[:, cs] = proj(OFF_VB, c)


def _proj_call(x, gpre, w_in, bgate, lng, lnb, ws, bst, wpa):
    m = x.shape[0]
    tm = TM_PROJ
    row = lambda i: (i, 0)
    out_f32 = jax.ShapeDtypeStruct((m, D_B), F32)
    out_bf16 = jax.ShapeDtypeStruct((m, D_B), BF16)
    return pl.pallas_call(
        _proj_kernel,
        grid=(m // tm,),
        in_specs=[
            pl.BlockSpec((tm, D_MODEL), row),
            _const_spec((1, D_MODEL)),
            _const_spec((D_MODEL, N_IN)),
            _const_spec((1, 2 * D_MODEL)),
            _const_spec((1, D_A)),
            _const_spec((1, D_A)),
            _const_spec((G_A, CHUNK, CHUNK)),
            _const_spec((CHUNK, G_A)),
            _const_spec((D_A, D_MODEL)),
        ],
        out_specs=[pl.BlockSpec((tm, D_B), row)] * 6,
        out_shape=[out_f32, out_f32, out_f32, out_bf16, out_bf16, out_f32],
        scratch_shapes=[
            pltpu.VMEM((tm, D_A), F32),
            pltpu.VMEM((tm, D_A), F32),
            pltpu.VMEM((tm, D_A), BF16),
            pltpu.VMEM((tm, D_A), BF16),
        ],
        compiler_params=pltpu.CompilerParams(
            dimension_semantics=("arbitrary",), vmem_limit_bytes=56 * MIB),
        name="proj_gmlp",
    )(x, gpre, w_in, bgate, lng, lnb, ws, bst, wpa)


def _attn_kernel(q_ref, k_ref, v_ref, y_ref, qd, kd, vd, f4q, f4k, f4v, o_scr, d_scr, m_scr,
                 s_scr, p_scr, bias_scr):
    seq = q_ref.shape[0]
    n_pat = len(DILATIONS)
    blocks_per_pat = seq // QBLK
    log2_blocks = blocks_per_pat.bit_length() - 1
    head = pl.program_id(1)

    slope = LOG2E * jnp.exp2(jnp.full((QBLK, KBLK), -1.0, F32) * (head + 1).astype(F32))
    qi = lax.broadcasted_iota(jnp.int32, (QBLK, KBLK), 0)
    ki = lax.broadcasted_iota(jnp.int32, (QBLK, KBLK), 1)
    for p, d in enumerate(DILATIONS):
        for case, shift in enumerate((N_SIDE, 0, KBLK - QBLK)):
            rel = jnp.abs(ki - shift - qi)
            bias_scr[3 * p + case] = jnp.where(rel <= N_SIDE, -slope * (d * rel).astype(F32), NEG)

    assert DILATIONS == (1, 4, 16)
    rows_per_copy = 256
    n_copies = seq // rows_per_copy
    srcs = (q_ref, k_ref, v_ref)
    dsts = (qd, kd, vd)
    mids = (f4q, f4k, f4v)

    def deinterleave(i, carry):
        dst = pl.ds(pl.multiple_of(i * rows_per_copy, rows_per_copy), rows_per_copy)
        src4 = pl.ds(i // 4 + 4 * rows_per_copy * (i % 4), rows_per_copy, stride=4)
        for src, mid, out in zip(srcs, mids, dsts):
            out[0, dst, :] = src[dst, :].astype(BF16)
            x4 = src[src4, :]
            mid[dst, :] = x4
            out[1, dst, :] = x4.astype(BF16)
        return carry

    lax.fori_loop(0, n_copies, deinterleave, 0)

    def deinterleave16(i, carry):
        dst = pl.ds(pl.multiple_of(i * rows_per_copy, rows_per_copy), rows_per_copy)
        src = pl.ds((i % 4) * (seq // 4) + i // 4, rows_per_copy, stride=4)
        for mid, out in zip(mids, dsts):
            out[2, dst, :] = mid[src, :].astype(BF16)
        return carry

    assert seq // 16 == rows_per_copy
    lax.fori_loop(0, n_copies, deinterleave16, 0)

    def block_addr(g):
        p = lax.shift_right_logical(g, jnp.int32(log2_blocks))
        j = g & (blocks_per_pat - 1)
        log2_per_res = log2_blocks - 2 * p
        per_res = lax.shift_left(jnp.int32(1), log2_per_res)
        r = lax.shift_right_logical(j, log2_per_res)
        n = j & (per_res - 1)
        sub_len = lax.shift_right_logical(jnp.int32(seq), 2 * p)
        case = jnp.where(n == 0, 1, jnp.where(n == per_res - 1, 2, 0))
        k0 = r * sub_len + jnp.clip(QBLK * n - N_SIDE, 0, sub_len - KBLK)
        return p, pl.multiple_of(j * QBLK, QBLK), pl.multiple_of(k0, N_SIDE), 3 * p + case

    def scores_stage(grp):
        slot = grp & 1
        for b in range(GROUP):
            p, q0, k0, _ = block_addr(grp * GROUP + b)
            s_scr[slot, b * QBLK:(b + 1) * QBLK, :] = lax.dot_general(
                qd[p, pl.ds(q0, QBLK), :], kd[p, pl.ds(k0, KBLK), :],
                (((1,), (1,)), ((), ())), preferred_element_type=F32)

    def softmax_stage(grp):
        slot = grp & 1
        for b in range(GROUP):
            p, q0, _, bias_idx = block_addr(grp * GROUP + b)
            rows = slice(b * QBLK, (b + 1) * QBLK)
            s = s_scr[slot, rows, :] + bias_scr[bias_idx]
            mx = jnp.max(s, axis=-1, keepdims=True)
            p_scr[slot, rows, :] = jnp.exp2(s - mx).astype(BF16)
            m_scr[p, pl.ds(q0, QBLK), :] = jnp.broadcast_to(mx, (QBLK, HEAD_DIM))

    ones = jnp.ones((KBLK, HEAD_DIM), BF16)

    def values_stage(grp):
        slot = grp & 1
        for b in range(GROUP):
            p, q0, k0, _ = block_addr(grp * GROUP + b)
            v_aug = jnp.concatenate([vd[p, pl.ds(k0, KBLK), :], ones], axis=1)
            oa = jnp.dot(p_scr[slot, b * QBLK:(b + 1) * QBLK, :], v_aug, preferred_element_type=F32)
            o_scr[p, pl.ds(q0, QBLK), :] = oa[:, :HEAD_DIM]
            d_scr[p, pl.ds(q0, QBLK), :] = oa[:, HEAD_DIM:]

    n_groups = n_pat * blocks_per_pat // GROUP
    scores_stage(jnp.int32(0))
    scores_stage(jnp.int32(1))
    softmax_stage(jnp.int32(0))

    def pipelined(t, carry):
        values_stage(t - 2)
        softmax_stage(t - 1)
        scores_stage(t)
        return carry

    lax.fori_loop(2, n_groups, pipelined, 0)
    softmax_stage(jnp.int32(n_groups - 1))
    values_stage(jnp.int32(n_groups - 2))
    values_stage(jnp.int32(n_groups - 1))

    rows = 64
    n_chunks = seq // rows
    o16c, d16c, m16c = f4q, f4k, f4v

    def to_order4(i, carry):
        r16 = i // (seq // 16 // rows)
        a0 = (i % (seq // 16 // rows)) * rows
        src = pl.ds(pl.multiple_of(i * rows, rows), rows)
        dst = pl.ds((r16 % 4) * (seq // 4) + r16 // 4 + 4 * a0, rows, stride=4)
        o16c[dst, :] = o_scr[2, src, :]
        d16c[dst, :] = d_scr[2, src, :]
        m16c[dst, :] = m_scr[2, src, :]
        return carry

    lax.fori_loop(0, n_chunks, to_order4, 0, unroll=2)

    def combine(i, carry):
        r4 = i // (seq // 4 // rows)
        m0 = (i % (seq // 4 // rows)) * rows
        s4 = pl.ds(pl.multiple_of(i * rows, rows), rows)
        s1 = pl.ds(r4 + 4 * m0, rows, stride=4)
        a1, a4, a16 = m_scr[0, s1, :], m_scr[1, s4, :], m16c[s4, :]
        mx = jnp.maximum(jnp.maximum(a1, a4), a16)
        w1, w4, w16 = jnp.exp2(a1 - mx), jnp.exp2(a4 - mx), jnp.exp2(a16 - mx)
        num = w1 * o_scr[0, s1, :] + w4 * o_scr[1, s4, :] + w16 * o16c[s4, :]
        den = w1 * d_scr[0, s1, :] + w4 * d_scr[1, s4, :] + w16 * d16c[s4, :]
        o_scr[2, s1, :] = num / den
        return carry

    lax.fori_loop(0, n_chunks, combine, 0, unroll=2)
    y_ref[...] = o_scr[2].astype(BF16)


def _attn_call(q, k, v, batch, seq):
    n_pat = len(DILATIONS)
    blk = pl.BlockSpec((seq, HEAD_DIM), lambda b, h: (b, h))
    pat_f32 = pltpu.VMEM((n_pat, seq, HEAD_DIM), F32)
    pat_bf16 = pltpu.VMEM((n_pat, seq, HEAD_DIM), BF16)
    return pl.pallas_call(
        _attn_kernel,
        grid=(batch, N_HEADS),
        in_specs=[blk, blk, blk],
        out_specs=blk,
        out_shape=jax.ShapeDtypeStruct((batch * seq, D_B), BF16),
        scratch_shapes=[pat_bf16] * 3 + [pltpu.VMEM((seq, HEAD_DIM), F32)] * 3 + [pat_f32] * 3
        + [pltpu.VMEM((2, GROUP * QBLK, KBLK), F32),
           pltpu.VMEM((2, GROUP * QBLK, KBLK), BF16),
           pltpu.VMEM((3 * n_pat, QBLK, KBLK), F32)],
        compiler_params=pltpu.CompilerParams(
            dimension_semantics=("arbitrary", "arbitrary"), vmem_limit_bytes=58 * MIB),
        name="dilated_attn",
    )(q, k, v)


def _out_kernel(yb_ref, zb_ref, gb_ref, ma_ref, x_ref, wpb_ref, wo_ref, gpost_ref, o_ref):
    yb = (yb_ref[...].astype(F32) * zb_ref[...].astype(F32)).astype(BF16)
    mb = jnp.dot(yb, wpb_ref[...], preferred_element_type=F32)
    m = ma_ref[...] + gb_ref[...].astype(F32) * mb
    r = jnp.dot(m.astype(BF16), wo_ref[...], preferred_element_type=F32)
    inv = lax.rsqrt(jnp.mean(r * r, axis=-1, keepdims=True) + EPS)
    o_ref[...] = x_ref[...] + r * inv * gpost_ref[...]


def _out_call(yb, zb, gb, ma, x, wpb, wo, gpost):
    m = x.shape[0]
    tm = TM_OUT
    row = pl.BlockSpec((tm, D_MODEL), lambda i: (i, 0))
    return pl.pallas_call(
        _out_kernel,
        grid=(m // tm,),
        in_specs=[row, row, row, row, row,
                  _const_spec((D_B, D_MODEL)), _const_spec((D_MODEL, D_MODEL)),
                  _const_spec((1, D_MODEL))],
        out_specs=row,
        out_shape=jax.ShapeDtypeStruct((m, D_MODEL), F32),
        compiler_params=pltpu.CompilerParams(
            dimension_semantics=("arbitrary",), vmem_limit_bytes=40 * MIB),
        name="merge_out",
    )(yb, zb, gb, ma, x, wpb, wo, gpost)


def kernel(x, w_in, b_gate, g_pre, g_post, sgu_ln_g, sgu_ln_b, w_spatial, b_spatial,
           w_proj_a, w_proj_b, w_out):
    batch, seq, d_model = x.shape
    depth = w_in.shape[0]
    assert d_model == D_MODEL and w_in.shape[2] == N_IN
    assert seq % (16 * 256) == 0 and (batch * seq) % TM_OUT == 0
    xf = x.reshape(batch * seq, d_model)
    for l in range(depth):
        q, k, v, zb, gb, ma = _proj_call(
            xf, g_pre[l][None], w_in[l].astype(BF16), b_gate[l][None],
            sgu_ln_g[l][None], sgu_ln_b[l][None], w_spatial[l].astype(BF16),
            b_spatial[l].T, w_proj_a[l].astype(BF16))
        yb = _attn_call(q, k, v, batch, seq)
        xf = _out_call(yb, zb, gb, ma, xf, w_proj_b[l].astype(BF16), w_out[l].astype(BF16),
                       g_post[l][None])
    return xf.reshape(batch, seq, d_model)
```

```python
import functools
import math

import jax
import jax.numpy as jnp
from jax import lax
from jax.experimental import pallas as pl
from jax.experimental.pallas import tpu as pltpu

D_MODEL = 1024
D_A = 2 * D_MODEL
CHUNK = 128
G_A = D_A // 128
N_HEADS = 8
HEAD_DIM = 128
D_B = N_HEADS * HEAD_DIM
DILATIONS = (1, 4, 16)
N_SIDE = 64
NEG = -1e30
EPS = 1e-6
LOG2E = math.log2(math.e)

OFF_U = 0
OFF_V = OFF_U + D_A
OFF_ZA = OFF_V + D_A
OFF_Q = OFF_ZA + D_A
OFF_K = OFF_Q + D_B
OFF_VB = OFF_K + D_B
OFF_ZB = OFF_VB + D_B
OFF_GA = OFF_ZB + D_B
OFF_GB = OFF_GA + D_MODEL
N_IN = OFF_GB + D_MODEL

QBLK = 128
KBLK = QBLK + 2 * N_SIDE
GROUP = 8
TM_PROJ = 256
TM_OUT = 512
CW = 512
MIB = 1024 * 1024

F32 = jnp.float32
BF16 = jnp.bfloat16


def _gelu(x):
    c = math.sqrt(2.0 / math.pi)
    return 0.5 * x * (1.0 + jnp.tanh(c * (x + 0.044715 * (x * x * x))))


def _sigmoid(x):
    return 0.5 * (jnp.tanh(0.5 * x) + 1.0)


def _silu(x):
    return x * _sigmoid(x)


def _const_spec(shape):
    nd = len(shape)
    return pl.BlockSpec(shape, lambda *_: (0,) * nd, pipeline_mode=pl.Buffered(1))


def _proj_kernel(x_ref, gpre_ref, w_in_ref, bgate_ref, lng_ref, lnb_ref, ws_ref, bst_ref, wpa_ref,
                 q_ref, k_ref, v_ref, zb_ref, gb_ref, ma_ref,
                 t_scr, v_scr, vn_scr, ya_scr):
    tm = x_ref.shape[0]
    x = x_ref[...]
    inv = lax.rsqrt(jnp.mean(x * x, axis=-1, keepdims=True) + EPS)
    h = (x * inv * gpre_ref[...]).astype(BF16)

    def proj(off, c):
        return jnp.dot(h, w_in_ref[:, off + c:off + c + CW], preferred_element_type=F32)

    for c in range(0, D_A, CW):
        v_scr[:, c:c + CW] = _gelu(proj(OFF_V, c))

    v = v_scr[...]
    mu = jnp.mean(v, axis=-1, keepdims=True)
    dv = v - mu
    var = jnp.mean(dv * dv, axis=-1, keepdims=True)
    vn_scr[...] = (dv * lax.rsqrt(var + EPS) * lng_ref[...] + lnb_ref[...]).astype(BF16)

    scale = LOG2E * HEAD_DIM ** -0.5

    def attn_branch(c):
        cs = slice(c, c + CW)
        gb_ref[:, cs] = _sigmoid(proj(OFF_GB, c) + bgate_ref[:, D_MODEL + c:D_MODEL + c + CW]).astype(BF16)
        zb_ref[:, cs] = _silu(proj(OFF_ZB, c)).astype(BF16)
        q_ref[:, cs] = proj(OFF_Q, c) * scale
        k_ref[:, cs] = proj(OFF_K, c)
        v_ref[:, cs] = proj(OFF_VB, c)

    for c in range(0, D_A, CW):
        t_scr[:, c:c + CW] = _gelu(proj(OFF_U, c)) * _silu(proj(OFF_ZA, c))
        if c % (2 * CW) == 0:
            attn_branch(c // 2)

    for c in range(tm // CHUNK):
        rows = slice(c * CHUNK, (c + 1) * CHUNK)
        for g in range(G_A):
            cols = slice(g * 128, (g + 1) * 128)
            s = jnp.dot(ws_ref[g], vn_scr[rows, cols], preferred_element_type=F32)
            s = s + bst_ref[:, g:g + 1]
            ya_scr[rows, cols] = (t_scr[rows, cols] * s).astype(BF16)

    for c in range(0, D_MODEL, CW):
        cs = slice(c, c + CW)
        ya_p = jnp.dot(ya_scr[...], wpa_ref[:, cs], preferred_element_type=F32)
        ma_ref[:, cs] = _sigmoid(proj(OFF_GA, c) + bgate_ref[:, cs]) * ya_p


def _proj_call(x, gpre, w_in, bgate, lng, lnb, ws, bst, wpa):
    m = x.shape[0]
    tm = TM_PROJ
    row = lambda i: (i, 0)
    out_f32 = jax.ShapeDtypeStruct((m, D_B), F32)
    out_bf16 = jax.ShapeDtypeStruct((m, D_B), BF16)
    return pl.pallas_call(
        _proj_kernel,
        grid=(m // tm,),
        in_specs=[
            pl.BlockSpec((tm, D_MODEL), row),
            _const_spec((1, D_MODEL)),
            _const_spec((D_MODEL, N_IN)),
            _const_spec((1, 2 * D_MODEL)),
            _const_spec((1, D_A)),
            _const_spec((1, D_A)),
            _const_spec((G_A, CHUNK, CHUNK)),
            _const_spec((CHUNK, G_A)),
            _const_spec((D_A, D_MODEL)),
        ],
        out_specs=[pl.BlockSpec((tm, D_B), row)] * 6,
        out_shape=[out_f32, out_f32, out_f32, out_bf16, out_bf16, out_f32],
        scratch_shapes=[
            pltpu.VMEM((tm, D_A), F32),
            pltpu.VMEM((tm, D_A), F32),
            pltpu.VMEM((tm, D_A), BF16),
            pltpu.VMEM((tm, D_A), BF16),
        ],
        compiler_params=pltpu.CompilerParams(
            dimension_semantics=("arbitrary",), vmem_limit_bytes=56 * MIB),
        name="proj_gmlp",
    )(x, gpre, w_in, bgate, lng, lnb, ws, bst, wpa)


def _attn_kernel(q_ref, k_ref, v_ref, y_ref, qd, kd, vd, f4q, f4k, f4v, o_scr, d_scr, m_scr,
                 s_scr, p_scr, bias_scr):
    seq = q_ref.shape[0]
    n_pat = len(DILATIONS)
    blocks_per_pat = seq // QBLK
    log2_blocks = blocks_per_pat.bit_length() - 1
    head = pl.program_id(1)

    slope = LOG2E * jnp.exp2(jnp.full((QBLK, KBLK), -1.0, F32) * (head + 1).astype(F32))
    qi = lax.broadcasted_iota(jnp.int32, (QBLK, KBLK), 0)
    ki = lax.broadcasted_iota(jnp.int32, (QBLK, KBLK), 1)
    for p, d in enumerate(DILATIONS):
        for case, shift in enumerate((N_SIDE, 0, KBLK - QBLK)):
            rel = jnp.abs(ki - shift - qi)
            bias_scr[3 * p + case] = jnp.where(rel <= N_SIDE, -slope * (d * rel).astype(F32), NEG)

    assert DILATIONS == (1, 4, 16)
    rows_per_copy = 256
    n_copies = seq // rows_per_copy
    srcs = (q_ref, k_ref, v_ref)
    dsts = (qd, kd, vd)
    mids = (f4q, f4k, f4v)

    def deinterleave(i, carry):
        dst = pl.ds(pl.multiple_of(i * rows_per_copy, rows_per_copy), rows_per_copy)
        src4 = pl.ds(i // 4 + 4 * rows_per_copy * (i % 4), rows_per_copy, stride=4)
        for src, mid, out in zip(srcs, mids, dsts):
            out[0, dst, :] = src[dst, :].astype(BF16)
            x4 = src[src4, :]
            mid[dst, :] = x4
            out[1, dst, :] = x4.astype(BF16)
        return carry

    lax.fori_loop(0, n_copies, deinterleave, 0)

    def deinterleave16(i, carry):
        dst = pl.ds(pl.multiple_of(i * rows_per_copy, rows_per_copy), rows_per_copy)
        src = pl.ds((i % 4) * (seq // 4) + i // 4, rows_per_copy, stride=4)
        for mid, out in zip(mids, dsts):
            out[2, dst, :] = mid[src, :].astype(BF16)
        return carry

    assert seq // 16 == rows_per_copy
    lax.fori_loop(0, n_copies, deinterleave16, 0)

    def block_addr(g):
        p = lax.shift_right_logical(g, jnp.int32(log2_blocks))
        j = g & (blocks_per_pat - 1)
        log2_per_res = log2_blocks - 2 * p
        per_res = lax.shift_left(jnp.int32(1), log2_per_res)
        r = lax.shift_right_logical(j, log2_per_res)
        n = j & (per_res - 1)
        sub_len = lax.shift_right_logical(jnp.int32(seq), 2 * p)
        case = jnp.where(n == 0, 1, jnp.where(n == per_res - 1, 2, 0))
        k0 = r * sub_len + jnp.clip(QBLK * n - N_SIDE, 0, sub_len - KBLK)
        return p, pl.multiple_of(j * QBLK, QBLK), pl.multiple_of(k0, N_SIDE), 3 * p + case

    def scores_stage(grp, slot):
        for b in range(GROUP):
            p, q0, k0, _ = block_addr(grp * GROUP + b)
            s_scr[slot, b * QBLK:(b + 1) * QBLK, :] = lax.dot_general(
                qd[p, pl.ds(q0, QBLK), :], kd[p, pl.ds(k0, KBLK), :],
                (((1,), (1,)), ((), ())), preferred_element_type=F32)

    def softmax_stage(grp, slot):
        for b in range(GROUP):
            p, q0, _, bias_idx = block_addr(grp * GROUP + b)
            rows = slice(b * QBLK, (b + 1) * QBLK)
            s = s_scr[slot, rows, :] + bias_scr[bias_idx]
            mx = jnp.max(s, axis=-1, keepdims=True)
            p_scr[slot, rows, :] = jnp.exp2(s - mx).astype(BF16)
            m_scr[p, pl.ds(q0, QBLK), :] = jnp.broadcast_to(mx, (QBLK, HEAD_DIM))

    ones = jnp.ones((KBLK, HEAD_DIM), BF16)

    def values_stage(grp, slot):
        for b in range(GROUP):
            p, q0, k0, _ = block_addr(grp * GROUP + b)
            v_aug = jnp.concatenate([vd[p, pl.ds(k0, KBLK), :], ones], axis=1)
            oa = jnp.dot(p_scr[slot, b * QBLK:(b + 1) * QBLK, :], v_aug, preferred_element_type=F32)
            o_scr[p, pl.ds(q0, QBLK), :] = oa[:, :HEAD_DIM]
            d_scr[p, pl.ds(q0, QBLK), :] = oa[:, HEAD_DIM:]

    n_groups = n_pat * blocks_per_pat // GROUP
    assert n_groups % 2 == 0
    scores_stage(jnp.int32(0), 0)
    scores_stage(jnp.int32(1), 1)
    softmax_stage(jnp.int32(0), 0)

    def pipelined(i, carry):
        t = 2 * i
        values_stage(t - 2, 0)
        softmax_stage(t - 1, 1)
        scores_stage(t, 0)
        values_stage(t - 1, 1)
        softmax_stage(t, 0)
        scores_stage(t + 1, 1)
        return carry

    lax.fori_loop(1, n_groups // 2, pipelined, 0)
    softmax_stage(jnp.int32(n_groups - 1), 1)
    values_stage(jnp.int32(n_groups - 2), 0)
    values_stage(jnp.int32(n_groups - 1), 1)

    rows = 64
    n_chunks = seq // rows
    o16c, d16c, m16c = f4q, f4k, f4v

    def to_order4(i, carry):
        r16 = i // (seq // 16 // rows)
        a0 = (i % (seq // 16 // rows)) * rows
        src = pl.ds(pl.multiple_of(i * rows, rows), rows)
        dst = pl.ds((r16 % 4) * (seq // 4) + r16 // 4 + 4 * a0, rows, stride=4)
        o16c[dst, :] = o_scr[2, src, :]
        d16c[dst, :] = d_scr[2, src, :]
        m16c[dst, :] = m_scr[2, src, :]
        return carry

    lax.fori_loop(0, n_chunks, to_order4, 0, unroll=2)

    def combine(i, carry):
        r4 = i // (seq // 4 // rows)
        m0 = (i % (seq // 4 // rows)) * rows
        s4 = pl.ds(pl.multiple_of(i * rows, rows), rows)
        s1 = pl.ds(r4 + 4 * m0, rows, stride=4)
        a1, a4, a16 = m_scr[0, s1, :], m_scr[1, s4, :], m16c[s4, :]
        mx = jnp.maximum(jnp.maximum(a1, a4), a16)
        w1, w4, w16 = jnp.exp2(a1 - mx), jnp.exp2(a4 - mx), jnp.exp2(a16 - mx)
        num = w1 * o_scr[0, s1, :] + w4 * o_scr[1, s4, :] + w16 * o16c[s4, :]
        den = w1 * d_scr[0, s1, :] + w4 * d_scr[1, s4, :] + w16 * d16c[s4, :]
        o_scr[2, s1, :] = num / den
        return carry

    lax.fori_loop(0, n_chunks, combine, 0, unroll=2)
    y_ref[...] = o_scr[2].astype(BF16)


def _attn_call(q, k, v, batch, seq):
    n_pat = len(DILATIONS)
    blk = pl.BlockSpec((seq, HEAD_DIM), lambda b, h: (b, h))
    pat_f32 = pltpu.VMEM((n_pat, seq, HEAD_DIM), F32)
    pat_bf16 = pltpu.VMEM((n_pat, seq, HEAD_DIM), BF16)
    return pl.pallas_call(
        _attn_kernel,
        grid=(batch, N_HEADS),
        in_specs=[blk, blk, blk],
        out_specs=blk,
        out_shape=jax.ShapeDtypeStruct((batch * seq, D_B), BF16),
        scratch_shapes=[pat_bf16] * 3 + [pltpu.VMEM((seq, HEAD_DIM), F32)] * 3 + [pat_f32] * 3
        + [pltpu.VMEM((2, GROUP * QBLK, KBLK), F32),
           pltpu.VMEM((2, GROUP * QBLK, KBLK), BF16),
           pltpu.VMEM((3 * n_pat, QBLK, KBLK), F32)],
        compiler_params=pltpu.CompilerParams(
            dimension_semantics=("arbitrary", "arbitrary"), vmem_limit_bytes=58 * MIB),
        name="dilated_attn",
    )(q, k, v)


def _out_kernel(yb_ref, zb_ref, gb_ref, ma_ref, x_ref, wpb_ref, wo_ref, gpost_ref, o_ref):
    yb = (yb_ref[...].astype(F32) * zb_ref[...].astype(F32)).astype(BF16)
    mb = jnp.dot(yb, wpb_ref[...], preferred_element_type=F32)
    m = ma_ref[...] + gb_ref[...].astype(F32) * mb
    r = jnp.dot(m.astype(BF16), wo_ref[...], preferred_element_type=F32)
    inv = lax.rsqrt(jnp.mean(r * r, axis=-1, keepdims=True) + EPS)
    o_ref[...] = x_ref[...] + r * inv * gpost_ref[...]


def _out_call(yb, zb, gb, ma, x, wpb, wo, gpost):
    m = x.shape[0]
    tm = TM_OUT
    row = pl.BlockSpec((tm, D_MODEL), lambda i: (i, 0))
    return pl.pallas_call(
        _out_kernel,
        grid=(m // tm,),
        in_specs=[row, row, row, row, row,
                  _const_spec((D_B, D_MODEL)), _const_spec((D_MODEL, D_MODEL)),
                  _const_spec((1, D_MODEL))],
        out_specs=row,
        out_shape=jax.ShapeDtypeStruct((m, D_MODEL), F32),
        compiler_params=pltpu.CompilerParams(
            dimension_semantics=("arbitrary",), vmem_limit_bytes=40 * MIB),
        name="merge_out",
    )(yb, zb, gb, ma, x, wpb, wo, gpost)


def kernel(x, w_in, b_gate, g_pre, g_post, sgu_ln_g, sgu_ln_b, w_spatial, b_spatial,
           w_proj_a, w_proj_b, w_out):
    batch, seq, d_model = x.shape
    depth = w_in.shape[0]
    assert d_model == D_MODEL and w_in.shape[2] == N_IN
    assert seq % (16 * 256) == 0 and (batch * seq) % TM_OUT == 0
    xf = x.reshape(batch * seq, d_model)
    for l in range(depth):
        q, k, v, zb, gb, ma = _proj_call(
            xf, g_pre[l][None], w_in[l].astype(BF16), b_gate[l][None],
            sgu_ln_g[l][None], sgu_ln_b[l][None], w_spatial[l].astype(BF16),
            b_spatial[l].T, w_proj_a[l].astype(BF16))
        yb = _attn_call(q, k, v, batch, seq)
        xf = _out_call(yb, zb, gb, ma, xf, w_proj_b[l].astype(BF16), w_out[l].astype(BF16),
                       g_post[l][None])
    return xf.reshape(batch, seq, d_model)
```

```python
import functools
import math

import jax
import jax.numpy as jnp
from jax import lax
from jax.experimental import pallas as pl
from jax.experimental.pallas import tpu as pltpu

D_MODEL = 1024
D_A = 2 * D_MODEL
CHUNK = 128
G_A = D_A // 128
N_HEADS = 8
HEAD_DIM = 128
D_B = N_HEADS * HEAD_DIM
DILATIONS = (1, 4, 16)
N_SIDE = 64
NEG = -1e30
EPS = 1e-6
LOG2E = math.log2(math.e)

OFF_U = 0
OFF_V = OFF_U + D_A
OFF_ZA = OFF_V + D_A
OFF_Q = OFF_ZA + D_A
OFF_K = OFF_Q + D_B
OFF_VB = OFF_K + D_B
OFF_ZB = OFF_VB + D_B
OFF_GA = OFF_ZB + D_B
OFF_GB = OFF_GA + D_MODEL
N_IN = OFF_GB + D_MODEL

QBLK = 128
KBLK = QBLK + 2 * N_SIDE
GROUP = 8
TM_PROJ = 256
TM_OUT = 512
CW = 512
MIB = 1024 * 1024

F32 = jnp.float32
BF16 = jnp.bfloat16


def _gelu(x):
    c = math.sqrt(2.0 / math.pi)
    return 0.5 * x * (1.0 + jnp.tanh(c * (x + 0.044715 * (x * x * x))))


def _sigmoid(x):
    return 0.5 * (jnp.tanh(0.5 * x) + 1.0)


def _silu(x):
    return x * _sigmoid(x)


def _const_spec(shape):
    nd = len(shape)
    return pl.BlockSpec(shape, lambda *_: (0,) * nd, pipeline_mode=pl.Buffered(1))


def _proj_kernel(x_ref, gpre_ref, w_in_ref, bgate_ref, lng_ref, lnb_ref, ws_ref, bst_ref, wpa_ref,
                 q_ref, k_ref, v_ref, zb_ref, gb_ref, ma_ref,
                 t_scr, v_scr, vn_scr, ya_scr):
    tm = x_ref.shape[0]
    x = x_ref[...]
    inv = lax.rsqrt(jnp.mean(x * x, axis=-1, keepdims=True) + EPS)
    h = (x * inv * gpre_ref[...]).astype(BF16)

    def proj(off, c):
        return jnp.dot(h, w_in_ref[:, off + c:off + c + CW], preferred_element_type=F32)

    for c in range(0, D_A, CW):
        v_scr[:, c:c + CW] = _gelu(proj(OFF_V, c))

    v = v_scr[...]
    mu = jnp.mean(v, axis=-1, keepdims=True)
    dv = v - mu
    var = jnp.mean(dv * dv, axis=-1, keepdims=True)
    vn_scr[...] = (dv * lax.rsqrt(var + EPS) * lng_ref[...] + lnb_ref[...]).astype(BF16)

    scale = LOG2E * HEAD_DIM ** -0.5

    def attn_branch(c):
        cs = slice(c, c + CW)
        gb_ref[:, cs] = _sigmoid(proj(OFF_GB, c) + bgate_ref[:, D_MODEL + c:D_MODEL + c + CW]).astype(BF16)
        zb_ref[:, cs] = _silu(proj(OFF_ZB, c)).astype(BF16)
        q_ref[:, cs] = proj(OFF_Q, c) * scale
        k_ref[:, cs] = proj(OFF_K, c)
        v_ref[:, cs] = proj(OFF_VB, c)

    for c in range(0, D_A, CW):
        t_scr[:, c:c + CW] = _gelu(proj(OFF_U, c)) * _silu(proj(OFF_ZA, c))
        if c % (2 * CW) == 0:
            attn_branch(c // 2)

    assert tm % (2 * CHUNK) == 0
    for c in range(0, tm, 2 * CHUNK):
        rows0 = slice(c, c + CHUNK)
        rows1 = slice(c + CHUNK, c + 2 * CHUNK)
        for g in range(G_A):
            cols = slice(g * 128, (g + 1) * 128)
            vn2 = jnp.concatenate([vn_scr[rows0, cols], vn_scr[rows1, cols]], axis=1)
            s2 = jnp.dot(ws_ref[g], vn2, preferred_element_type=F32) + bst_ref[:, g:g + 1]
            ya_scr[rows0, cols] = (t_scr[rows0, cols] * s2[:, :128]).astype(BF16)
            ya_scr[rows1, cols] = (t_scr[rows1, cols] * s2[:, 128:]).astype(BF16)

    for c in range(0, D_MODEL, CW):
        cs = slice(c, c + CW)
        ya_p = jnp.dot(ya_scr[...], wpa_ref[:, cs], preferred_element_type=F32)
        ma_ref[:, cs] = (_sigmoid(proj(OFF_GA, c) + bgate_ref[:, cs]) * ya_p).astype(BF16)


def _proj_call(x, gpre, w_in, bgate, lng, lnb, ws, bst, wpa):
    m = x.shape[0]
    tm = TM_PROJ
    row = lambda i: (i, 0)
    out_f32 = jax.ShapeDtypeStruct((m, D_B), F32)
    out_bf16 = jax.ShapeDtypeStruct((m, D_B), BF16)
    return pl.pallas_call(
        _proj_kernel,
        grid=(m // tm,),
        in_specs=[
            pl.BlockSpec((tm, D_MODEL), row),
            _const_spec((1, D_MODEL)),
            _const_spec((D_MODEL, N_IN)),
            _const_spec((1, 2 * D_MODEL)),
            _const_spec((1, D_A)),
            _const_spec((1, D_A)),
            _const_spec((G_A, CHUNK, CHUNK)),
            _const_spec((CHUNK, G_A)),
            _const_spec((D_A, D_MODEL)),
        ],
        out_specs=[pl.BlockSpec((tm, D_B), row)] * 6,
        out_shape=[out_f32, out_f32, out_f32, out_bf16, out_bf16, out_bf16],
        scratch_shapes=[
            pltpu.VMEM((tm, D_A), F32),
            pltpu.VMEM((tm, D_A), F32),
            pltpu.VMEM((tm, D_A), BF16),
            pltpu.VMEM((tm, D_A), BF16),
        ],
        compiler_params=pltpu.CompilerParams(
            dimension_semantics=("arbitrary",), vmem_limit_bytes=56 * MIB),
        name="proj_gmlp",
    )(x, gpre, w_in, bgate, lng, lnb, ws, bst, wpa)


def _attn_kernel(q_ref, k_ref, v_ref, zb_ref, y_ref, qd, kd, vd, f4q, f4k, f4v, o_scr, d_scr, m_scr,
                 s_scr, p_scr, bias_scr):
    seq = q_ref.shape[0]
    n_pat = len(DILATIONS)
    blocks_per_pat = seq // QBLK
    log2_blocks = blocks_per_pat.bit_length() - 1
    head = pl.program_id(1)

    slope = LOG2E * jnp.exp2(jnp.full((QBLK, KBLK), -1.0, F32) * (head + 1).astype(F32))
    qi = lax.broadcasted_iota(jnp.int32, (QBLK, KBLK), 0)
    ki = lax.broadcasted_iota(jnp.int32, (QBLK, KBLK), 1)
    for p, d in enumerate(DILATIONS):
        for case, shift in enumerate((N_SIDE, 0, KBLK - QBLK)):
            rel = jnp.abs(ki - shift - qi)
            bias_scr[3 * p + case] = jnp.where(rel <= N_SIDE, -slope * (d * rel).astype(F32), NEG)

    assert DILATIONS == (1, 4, 16)
    rows_per_copy = 256
    n_copies = seq // rows_per_copy
    srcs = (q_ref, k_ref, v_ref)
    dsts = (qd, kd, vd)
    mids = (f4q, f4k, f4v)

    def deinterleave(i, carry):
        dst = pl.ds(pl.multiple_of(i * rows_per_copy, rows_per_copy), rows_per_copy)
        src4 = pl.ds(i // 4 + 4 * rows_per_copy * (i % 4), rows_per_copy, stride=4)
        for src, mid, out in zip(srcs, mids, dsts):
            out[0, dst, :] = src[dst, :].astype(BF16)
            x4 = src[src4, :]
            mid[dst, :] = x4
            out[1, dst, :] = x4.astype(BF16)
        return carry

    lax.fori_loop(0, n_copies, deinterleave, 0)

    def deinterleave16(i, carry):
        dst = pl.ds(pl.multiple_of(i * rows_per_copy, rows_per_copy), rows_per_copy)
        src = pl.ds((i % 4) * (seq // 4) + i // 4, rows_per_copy, stride=4)
        for mid, out in zip(mids, dsts):
            out[2, dst, :] = mid[src, :].astype(BF16)
        return carry

    assert seq // 16 == rows_per_copy
    lax.fori_loop(0, n_copies, deinterleave16, 0)

    def block_addr(g):
        p = lax.shift_right_logical(g, jnp.int32(log2_blocks))
        j = g & (blocks_per_pat - 1)
        log2_per_res = log2_blocks - 2 * p
        per_res = lax.shift_left(jnp.int32(1), log2_per_res)
        r = lax.shift_right_logical(j, log2_per_res)
        n = j & (per_res - 1)
        sub_len = lax.shift_right_logical(jnp.int32(seq), 2 * p)
        case = jnp.where(n == 0, 1, jnp.where(n == per_res - 1, 2, 0))
        k0 = r * sub_len + jnp.clip(QBLK * n - N_SIDE, 0, sub_len - KBLK)
        return p, pl.multiple_of(j * QBLK, QBLK), pl.multiple_of(k0, N_SIDE), 3 * p + case

    def scores_stage(grp, slot):
        for b in range(GROUP):
            p, q0, k0, _ = block_addr(grp * GROUP + b)
            s_scr[slot, b * QBLK:(b + 1) * QBLK, :] = lax.dot_general(
                qd[p, pl.ds(q0, QBLK), :], kd[p, pl.ds(k0, KBLK), :],
                (((1,), (1,)), ((), ())), preferred_element_type=F32)

    def softmax_stage(grp, slot):
        for b in range(GROUP):
            p, q0, _, bias_idx = block_addr(grp * GROUP + b)
            rows = slice(b * QBLK, (b + 1) * QBLK)
            s = s_scr[slot, rows, :] + bias_scr[bias_idx]
            mx = jnp.max(s, axis=-1, keepdims=True)
            p_scr[slot, rows, :] = jnp.exp2(s - mx).astype(BF16)
            m_scr[p, pl.ds(q0, QBLK), :] = jnp.broadcast_to(mx, (QBLK, HEAD_DIM))

    ones = jnp.ones((KBLK, HEAD_DIM), BF16)

    def values_stage(grp, slot):
        for b in range(GROUP):
            p, q0, k0, _ = block_addr(grp * GROUP + b)
            v_aug = jnp.concatenate([vd[p, pl.ds(k0, KBLK), :], ones], axis=1)
            oa = jnp.dot(p_scr[slot, b * QBLK:(b + 1) * QBLK, :], v_aug, preferred_element_type=F32)
            o_scr[p, pl.ds(q0, QBLK), :] = oa[:, :HEAD_DIM]
            d_scr[p, pl.ds(q0, QBLK), :] = oa[:, HEAD_DIM:]

    n_groups = n_pat * blocks_per_pat // GROUP
    assert n_groups % 2 == 0
    scores_stage(jnp.int32(0), 0)
    scores_stage(jnp.int32(1), 1)
    softmax_stage(jnp.int32(0), 0)

    def pipelined(i, carry):
        t = 2 * i
        values_stage(t - 2, 0)
        softmax_stage(t - 1, 1)
        scores_stage(t, 0)
        values_stage(t - 1, 1)
        softmax_stage(t, 0)
        scores_stage(t + 1, 1)
        return carry

    lax.fori_loop(1, n_groups // 2, pipelined, 0)
    softmax_stage(jnp.int32(n_groups - 1), 1)
    values_stage(jnp.int32(n_groups - 2), 0)
    values_stage(jnp.int32(n_groups - 1), 1)

    rows = 64
    n_chunks = seq // rows
    o16c, d16c, m16c = f4q, f4k, f4v

    def to_order4(i, carry):
        r16 = i // (seq // 16 // rows)
        a0 = (i % (seq // 16 // rows)) * rows
        src = pl.ds(pl.multiple_of(i * rows, rows), rows)
        dst = pl.ds((r16 % 4) * (seq // 4) + r16 // 4 + 4 * a0, rows, stride=4)
        o16c[dst, :] = o_scr[2, src, :]
        d16c[dst, :] = d_scr[2, src, :]
        m16c[dst, :] = m_scr[2, src, :]
        return carry

    lax.fori_loop(0, n_chunks, to_order4, 0, unroll=2)

    def combine(i, carry):
        r4 = i // (seq // 4 // rows)
        m0 = (i % (seq // 4 // rows)) * rows
        s4 = pl.ds(pl.multiple_of(i * rows, rows), rows)
        s1 = pl.ds(r4 + 4 * m0, rows, stride=4)
        a1, a4, a16 = m_scr[0, s1, :], m_scr[1, s4, :], m16c[s4, :]
        mx = jnp.maximum(jnp.maximum(a1, a4), a16)
        w1, w4, w16 = jnp.exp2(a1 - mx), jnp.exp2(a4 - mx), jnp.exp2(a16 - mx)
        num = w1 * o_scr[0, s1, :] + w4 * o_scr[1, s4, :] + w16 * o16c[s4, :]
        den = w1 * d_scr[0, s1, :] + w4 * d_scr[1, s4, :] + w16 * d16c[s4, :]
        o_scr[2, s1, :] = num / den
        return carry

    lax.fori_loop(0, n_chunks, combine, 0, unroll=2)
    y_ref[...] = (o_scr[2] * zb_ref[...].astype(F32)).astype(BF16)


def _attn_call(q, k, v, zb, batch, seq):
    n_pat = len(DILATIONS)
    blk = pl.BlockSpec((seq, HEAD_DIM), lambda b, h: (b, h))
    pat_f32 = pltpu.VMEM((n_pat, seq, HEAD_DIM), F32)
    pat_bf16 = pltpu.VMEM((n_pat, seq, HEAD_DIM), BF16)
    return pl.pallas_call(
        _attn_kernel,
        grid=(batch, N_HEADS),
        in_specs=[blk, blk, blk, blk],
        out_specs=blk,
        out_shape=jax.ShapeDtypeStruct((batch * seq, D_B), BF16),
        scratch_shapes=[pat_bf16] * 3 + [pltpu.VMEM((seq, HEAD_DIM), F32)] * 3 + [pat_f32] * 3
        + [pltpu.VMEM((2, GROUP * QBLK, KBLK), F32),
           pltpu.VMEM((2, GROUP * QBLK, KBLK), BF16),
           pltpu.VMEM((3 * n_pat, QBLK, KBLK), F32)],
        compiler_params=pltpu.CompilerParams(
            dimension_semantics=("arbitrary", "arbitrary"), vmem_limit_bytes=58 * MIB),
        name="dilated_attn",
    )(q, k, v, zb)


def _out_kernel(yb_ref, gb_ref, ma_ref, x_ref, wpb_ref, wo_ref, gpost_ref, o_ref):
    mb = jnp.dot(yb_ref[...], wpb_ref[...], preferred_element_type=F32)
    m = ma_ref[...].astype(F32) + gb_ref[...].astype(F32) * mb
    r = jnp.dot(m.astype(BF16), wo_ref[...], preferred_element_type=F32)
    inv = lax.rsqrt(jnp.mean(r * r, axis=-1, keepdims=True) + EPS)
    o_ref[...] = x_ref[...] + r * inv * gpost_ref[...]


def _out_call(yb, gb, ma, x, wpb, wo, gpost):
    m = x.shape[0]
    tm = TM_OUT
    row = pl.BlockSpec((tm, D_MODEL), lambda i: (i, 0))
    return pl.pallas_call(
        _out_kernel,
        grid=(m // tm,),
        in_specs=[row, row, row, row,
                  _const_spec((D_B, D_MODEL)), _const_spec((D_MODEL, D_MODEL)),
                  _const_spec((1, D_MODEL))],
        out_specs=row,
        out_shape=jax.ShapeDtypeStruct((m, D_MODEL), F32),
        compiler_params=pltpu.CompilerParams(
            dimension_semantics=("arbitrary",), vmem_limit_bytes=40 * MIB),
        name="merge_out",
    )(yb, gb, ma, x, wpb, wo, gpost)


def kernel(x, w_in, b_gate, g_pre, g_post, sgu_ln_g, sgu_ln_b, w_spatial, b_spatial,
           w_proj_a, w_proj_b, w_out):
    batch, seq, d_model = x.shape
    depth = w_in.shape[0]
    assert d_model == D_MODEL and w_in.shape[2] == N_IN
    assert seq % (16 * 256) == 0 and (batch * seq) % TM_OUT == 0
    xf = x.reshape(batch * seq, d_model)
    for l in range(depth):
        q, k, v, zb, gb, ma = _proj_call(
            xf, g_pre[l][None], w_in[l].astype(BF16), b_gate[l][None],
            sgu_ln_g[l][None], sgu_ln_b[l][None], w_spatial[l].astype(BF16),
            b_spatial[l].T, w_proj_a[l].astype(BF16))
        yb = _attn_call(q, k, v, zb, batch, seq)
        xf = _out_call(yb, gb, ma, xf, w_proj_b[l].astype(BF16), w_out[l].astype(BF16),
                       g_post[l][None])
    return xf.reshape(batch, seq, d_model)
```

```python
import math

import jax
import numpy as np
import jax.numpy as jnp
from jax import lax
from jax.experimental import pallas as pl
from jax.experimental.pallas import tpu as pltpu

D_MODEL = 1024
D_A = 2 * D_MODEL
CHUNK = 128
G_A = D_A // 128
N_HEADS = 8
HEAD_DIM = 128
D_B = N_HEADS * HEAD_DIM
DILATIONS = (1, 4, 16)
N_SIDE = 64
NEG = -1e30
EPS = 1e-6
LOG2E = math.log2(math.e)

OFF_U = 0
OFF_V = OFF_U + D_A
OFF_ZA = OFF_V + D_A
OFF_Q = OFF_ZA + D_A
OFF_K = OFF_Q + D_B
OFF_VB = OFF_K + D_B
OFF_ZB = OFF_VB + D_B
OFF_GA = OFF_ZB + D_B
OFF_GB = OFF_GA + D_MODEL
N_IN = OFF_GB + D_MODEL

QBLK = 128
KBLK = QBLK + 2 * N_SIDE
PIECES = 4
QPIECE = QBLK // PIECES
KPIECE = KBLK // PIECES
GROUP = 8
TM_PROJ = 256
TM_OUT = 1024
OUT_SLAB = 256
CW = 512
MIB = 1024 * 1024

F32 = jnp.float32
BF16 = jnp.bfloat16


def _gelu(x):
    c = math.sqrt(2.0 / math.pi)
    return 0.5 * x * (1.0 + jnp.tanh(c * (x + 0.044715 * (x * x * x))))


def _sigmoid(x):
    return 0.5 * (jnp.tanh(0.5 * x) + 1.0)


def _silu(x):
    return x * _sigmoid(x)


def _const_spec(shape):
    nd = len(shape)
    return pl.BlockSpec(shape, lambda *_: (0,) * nd, pipeline_mode=pl.Buffered(1))


def _proj_kernel(x_ref, gpre_ref, w_in_ref, bgate_ref, lng_ref, lnb_ref, ws_ref, bst_ref, wpa_ref,
                 q4_ref, k4_ref, v4_ref, zb_ref, gb_ref, ma_ref,
                 t_scr, v_scr, vn_scr, ya_scr, slab_q, slab_k, slab_v):
    tm = x_ref.shape[0]
    x = x_ref[...]
    inv = lax.rsqrt(jnp.mean(x * x, axis=-1, keepdims=True) + EPS)
    h = (x * inv * gpre_ref[...]).astype(BF16)

    def proj(off, c):
        return jnp.dot(h, w_in_ref[:, off + c:off + c + CW], preferred_element_type=F32)

    for c in range(0, D_A, CW):
        v_scr[:, c:c + CW] = _gelu(proj(OFF_V, c))

    v = v_scr[...]
    mu = jnp.mean(v, axis=-1, keepdims=True)
    dv = v - mu
    var = jnp.mean(dv * dv, axis=-1, keepdims=True)
    vn_scr[...] = (dv * lax.rsqrt(var + EPS) * lng_ref[...] + lnb_ref[...]).astype(BF16)

    scale = LOG2E * HEAD_DIM ** -0.5
    qkv = ((OFF_Q, scale, slab_q, q4_ref), (OFF_K, None, slab_k, k4_ref), (OFF_VB, None, slab_v, v4_ref))

    def attn_branch(c):
        cs = slice(c, c + CW)
        gb_ref[:, cs] = _sigmoid(proj(OFF_GB, c) + bgate_ref[:, D_MODEL + c:D_MODEL + c + CW]).astype(BF16)
        zb_ref[:, cs] = _silu(proj(OFF_ZB, c)).astype(BF16)
        heads = range(c // HEAD_DIM, (c + CW) // HEAD_DIM)
        for off, mult, slab, out4 in qkv:
            res = proj(off, c)
            if mult is not None:
                res = res * mult
            for hd in heads:
                lo = hd * HEAD_DIM - c
                slab[hd] = res[:, lo:lo + HEAD_DIM]
            for hd in heads:
                for r in range(4):
                    out4[hd, r] = slab[hd, pl.ds(r, tm // 4, stride=4), :]

    for c in range(0, D_A, CW):
        t_scr[:, c:c + CW] = _gelu(proj(OFF_U, c)) * _silu(proj(OFF_ZA, c))
        if c % (2 * CW) == 0:
            attn_branch(c // 2)

    assert tm % (2 * CHUNK) == 0
    for c in range(0, tm, 2 * CHUNK):
        rows0 = slice(c, c + CHUNK)
        rows1 = slice(c + CHUNK, c + 2 * CHUNK)
        for g in range(G_A):
            cols = slice(g * 128, (g + 1) * 128)
            vn2 = jnp.concatenate([vn_scr[rows0, cols], vn_scr[rows1, cols]], axis=1)
            s2 = jnp.dot(ws_ref[g], vn2, preferred_element_type=F32) + bst_ref[:, g:g + 1]
            ya_scr[rows0, cols] = (t_scr[rows0, cols] * s2[:, :128]).astype(BF16)
            ya_scr[rows1, cols] = (t_scr[rows1, cols] * s2[:, 128:]).astype(BF16)

    for c in range(0, D_MODEL, CW):
        cs = slice(c, c + CW)
        ya_p = jnp.dot(ya_scr[...], wpa_ref[:, cs], preferred_element_type=F32)
        ma_ref[:, cs] = (_sigmoid(proj(OFF_GA, c) + bgate_ref[:, cs]) * ya_p).astype(BF16)


def _proj_call(x, gpre, w_in, bgate, lng, lnb, ws, bst, wpa):
    batch, seq, _ = x.shape
    tm = TM_PROJ
    rows = pl.BlockSpec((None, tm, D_MODEL), lambda b, i: (b, i, 0))
    by4 = pl.BlockSpec((None, N_HEADS, 4, tm // 4, HEAD_DIM), lambda b, i: (b, 0, 0, i, 0))
    shape4 = jax.ShapeDtypeStruct((batch, N_HEADS, 4, seq // 4, HEAD_DIM), F32)
    shape_rows = jax.ShapeDtypeStruct((batch, seq, D_B), BF16)
    return pl.pallas_call(
        _proj_kernel,
        grid=(batch, seq // tm),
        in_specs=[
            rows,
            _const_spec((1, D_MODEL)),
            _const_spec((D_MODEL, N_IN)),
            _const_spec((1, 2 * D_MODEL)),
            _const_spec((1, D_A)),
            _const_spec((1, D_A)),
            _const_spec((G_A, CHUNK, CHUNK)),
            _const_spec((CHUNK, G_A)),
            _const_spec((D_A, D_MODEL)),
        ],
        out_specs=[by4] * 3 + [rows] * 3,
        out_shape=[shape4] * 3 + [shape_rows] * 3,
        scratch_shapes=[
            pltpu.VMEM((tm, D_A), F32),
            pltpu.VMEM((tm, D_A), F32),
            pltpu.VMEM((tm, D_A), BF16),
            pltpu.VMEM((tm, D_A), BF16),
        ] + [pltpu.VMEM((N_HEADS, tm, HEAD_DIM), F32)] * 3,
        compiler_params=pltpu.CompilerParams(
            dimension_semantics=("arbitrary", "arbitrary"), vmem_limit_bytes=56 * MIB),
        name="proj_gmlp",
    )(x, gpre, w_in, bgate, lng, lnb, ws, bst, wpa)


def _block_table(seq):
    blocks_per_pat = seq // QBLK
    rows = []
    for p, d in enumerate(DILATIONS):
        sub_len = seq // d
        per_res = blocks_per_pat // d
        for j in range(blocks_per_pat):
            r, n = divmod(j, per_res)
            case = 1 if n == 0 else (2 if n == per_res - 1 else 0)
            first_q = QBLK * n
            first_k = min(max(first_q - N_SIDE, 0), sub_len - KBLK)
            if d == 1:
                q0, k0, q_stride, k_stride = first_q // 4, first_k // 4, seq // 4, seq // 4
            else:
                q0, k0, q_stride, k_stride = r * sub_len + first_q, r * sub_len + first_k, QPIECE, KPIECE
            rows.append([p, 1 if d == 16 else 0, q0, q_stride, k0, k_stride, 3 * p + case, 0])
    return np.asarray(rows, np.int32).reshape(-1)


TABLE_FIELDS = 8


def _attn_kernel(tbl_ref, q4_ref, k4_ref, v4_ref, zb_ref, y_ref,
                 qd, kd, vd, o16c, d16c, m16c, o_scr, d_scr, m_scr, s_scr, p_scr, bias_scr):
    seq = zb_ref.shape[0]
    n_pat = len(DILATIONS)
    assert DILATIONS == (1, 4, 16)
    blocks_per_pat = seq // QBLK
    head = pl.program_id(1)

    slope = LOG2E * jnp.exp2(jnp.full((QBLK, KBLK), -1.0, F32) * (head + 1).astype(F32))
    qi = lax.broadcasted_iota(jnp.int32, (QBLK, KBLK), 0)
    ki = lax.broadcasted_iota(jnp.int32, (QBLK, KBLK), 1)
    qi_pieces = 4 * (qi % QPIECE) + qi // QPIECE
    ki_pieces = 4 * (ki % KPIECE) + ki // KPIECE
    for p, d in enumerate(DILATIONS):
        qpos, kpos = (qi_pieces, ki_pieces) if d == 1 else (qi, ki)
        for case, shift in enumerate((N_SIDE, 0, KBLK - QBLK)):
            rel = jnp.abs(kpos - shift - qpos)
            bias_scr[3 * p + case] = jnp.where(rel <= N_SIDE, -slope * (d * rel).astype(F32), NEG)

    rows_per_copy = 256
    assert seq // 16 == rows_per_copy

    def to_bf16(i, carry):
        dst = pl.ds(pl.multiple_of(i * rows_per_copy, rows_per_copy), rows_per_copy)
        quarter = pl.ds(pl.multiple_of((i % 4) * rows_per_copy, rows_per_copy), rows_per_copy)
        for src, out in ((q4_ref, qd), (k4_ref, kd), (v4_ref, vd)):
            out[0, dst, :] = src[i // 4, quarter, :].astype(BF16)
            out[1, dst, :] = src[i % 4, pl.ds(i // 4, rows_per_copy, stride=4), :].astype(BF16)
        return carry

    lax.fori_loop(0, seq // rows_per_copy, to_bf16, 0)

    def block_addr(g):
        return tuple(tbl_ref[g * TABLE_FIELDS + f] for f in range(7))

    def pieces(ref, lead, first, stride, rows):
        return jnp.concatenate(
            [ref[lead, pl.ds(pl.multiple_of(first + i * stride, 16), rows), :] for i in range(PIECES)],
            axis=0)

    def scores_stage(grp, slot):
        for b in range(GROUP):
            _, layout, q0, q_stride, k0, k_stride, _ = block_addr(grp * GROUP + b)
            s_scr[slot, b * QBLK:(b + 1) * QBLK, :] = lax.dot_general(
                pieces(qd, layout, q0, q_stride, QPIECE), pieces(kd, layout, k0, k_stride, KPIECE),
                (((1,), (1,)), ((), ())), preferred_element_type=F32)

    def softmax_stage(grp, slot):
        for b in range(GROUP):
            p, _, q0, q_stride, _, _, bias_idx = block_addr(grp * GROUP + b)
            rows = slice(b * QBLK, (b + 1) * QBLK)
            s = s_scr[slot, rows, :] + bias_scr[bias_idx]
            mx = jnp.max(s, axis=-1, keepdims=True)
            p_scr[slot, rows, :] = jnp.exp2(s - mx).astype(BF16)
            for i in range(PIECES):
                m_scr[p, pl.ds(pl.multiple_of(q0 + i * q_stride, QPIECE), QPIECE), :] = jnp.broadcast_to(
                    mx[i * QPIECE:(i + 1) * QPIECE], (QPIECE, HEAD_DIM))

    ones = jnp.ones((KBLK, HEAD_DIM), BF16)

    def values_stage(grp, slot):
        for b in range(GROUP):
            p, layout, q0, q_stride, k0, k_stride, _ = block_addr(grp * GROUP + b)
            v_aug = jnp.concatenate([pieces(vd, layout, k0, k_stride, KPIECE), ones], axis=1)
            oa = jnp.dot(p_scr[slot, b * QBLK:(b + 1) * QBLK, :], v_aug, preferred_element_type=F32)
            for i in range(PIECES):
                dst = pl.ds(pl.multiple_of(q0 + i * q_stride, QPIECE), QPIECE)
                o_scr[p, dst, :] = oa[i * QPIECE:(i + 1) * QPIECE, :HEAD_DIM]
                d_scr[p, dst, :] = oa[i * QPIECE:(i + 1) * QPIECE, HEAD_DIM:]

    n_groups = n_pat * blocks_per_pat // GROUP
    assert n_groups % 2 == 0
    scores_stage(jnp.int32(0), 0)
    scores_stage(jnp.int32(1), 1)
    softmax_stage(jnp.int32(0), 0)

    def pipelined(i, carry):
        t = 2 * i
        values_stage(t - 2, 0)
        softmax_stage(t - 1, 1)
        scores_stage(t, 0)
        values_stage(t - 1, 1)
        softmax_stage(t, 0)
        scores_stage(t + 1, 1)
        return carry

    lax.fori_loop(1, n_groups // 2, pipelined, 0)
    softmax_stage(jnp.int32(n_groups - 1), 1)
    values_stage(jnp.int32(n_groups - 2), 0)
    values_stage(jnp.int32(n_groups - 1), 1)

    rows = 64
    n_chunks = seq // rows

    def to_order4(i, carry):
        r16 = i // (seq // 16 // rows)
        a0 = (i % (seq // 16 // rows)) * rows
        src = pl.ds(pl.multiple_of(i * rows, rows), rows)
        dst = pl.ds((r16 % 4) * (seq // 4) + r16 // 4 + 4 * a0, rows, stride=4)
        o16c[dst, :] = o_scr[2, src, :]
        d16c[dst, :] = d_scr[2, src, :]
        m16c[dst, :] = m_scr[2, src, :]
        return carry

    lax.fori_loop(0, n_chunks, to_order4, 0, unroll=2)

    def combine(i, carry):
        r4 = i // (seq // 4 // rows)
        m0 = (i % (seq // 4 // rows)) * rows
        s4 = pl.ds(pl.multiple_of(i * rows, rows), rows)
        a1, a4, a16 = m_scr[0, s4, :], m_scr[1, s4, :], m16c[s4, :]
        mx = jnp.maximum(jnp.maximum(a1, a4), a16)
        w1, w4, w16 = jnp.exp2(a1 - mx), jnp.exp2(a4 - mx), jnp.exp2(a16 - mx)
        num = w1 * o_scr[0, s4, :] + w4 * o_scr[1, s4, :] + w16 * o16c[s4, :]
        den = w1 * d_scr[0, s4, :] + w4 * d_scr[1, s4, :] + w16 * d16c[s4, :]
        o_scr[2, pl.ds(r4 + 4 * m0, rows, stride=4), :] = num / den
        return carry

    lax.fori_loop(0, n_chunks, combine, 0, unroll=2)
    y_ref[...] = (o_scr[2] * zb_ref[...].astype(F32)).astype(BF16)


def _attn_call(q4, k4, v4, zb):
    batch, seq, _ = zb.shape
    n_pat = len(DILATIONS)
    by4 = pl.BlockSpec((None, None, 4, seq // 4, HEAD_DIM), lambda b, h, tbl: (b, h, 0, 0, 0))
    nat = pl.BlockSpec((None, seq, HEAD_DIM), lambda b, h, tbl: (b, 0, h))
    seq_f32 = pltpu.VMEM((seq, HEAD_DIM), F32)
    pat_f32 = pltpu.VMEM((n_pat, seq, HEAD_DIM), F32)
    two_layouts = pltpu.VMEM((2, seq, HEAD_DIM), BF16)
    return pl.pallas_call(
        _attn_kernel,
        grid_spec=pltpu.PrefetchScalarGridSpec(
            num_scalar_prefetch=1,
            grid=(batch, N_HEADS),
            in_specs=[by4] * 3 + [nat],
            out_specs=nat,
            scratch_shapes=[two_layouts] * 3 + [seq_f32] * 3 + [pat_f32] * 3
            + [pltpu.VMEM((2, GROUP * QBLK, KBLK), F32),
               pltpu.VMEM((2, GROUP * QBLK, KBLK), BF16),
               pltpu.VMEM((3 * n_pat, QBLK, KBLK), F32)]),
        out_shape=jax.ShapeDtypeStruct((batch, seq, D_B), BF16),
        compiler_params=pltpu.CompilerParams(
            dimension_semantics=("arbitrary", "arbitrary"), vmem_limit_bytes=58 * MIB),
        name="dilated_attn",
    )(jnp.asarray(_block_table(seq)), q4, k4, v4, zb)


def _out_kernel(yb_ref, gb_ref, ma_ref, x_ref, wpb_ref, wo_ref, gpost_ref, o_ref):
    tm = o_ref.shape[0]
    for r0 in range(0, tm, OUT_SLAB):
        rows = slice(r0, r0 + OUT_SLAB)
        mb = jnp.dot(yb_ref[rows, :], wpb_ref[...], preferred_element_type=F32)
        m = ma_ref[rows, :].astype(F32) + gb_ref[rows, :].astype(F32) * mb
        r = jnp.dot(m.astype(BF16), wo_ref[...], preferred_element_type=F32)
        inv = lax.rsqrt(jnp.mean(r * r, axis=-1, keepdims=True) + EPS)
        o_ref[rows, :] = x_ref[rows, :] + r * inv * gpost_ref[...]


def _out_call(yb, gb, ma, x, wpb, wo, gpost):
    m = x.shape[0]
    tm = TM_OUT
    row = pl.BlockSpec((tm, D_MODEL), lambda i: (i, 0))
    return pl.pallas_call(
        _out_kernel,
        grid=(m // tm,),
        in_specs=[row, row, row, row,
                  _const_spec((D_B, D_MODEL)), _const_spec((D_MODEL, D_MODEL)),
                  _const_spec((1, D_MODEL))],
        out_specs=row,
        out_shape=jax.ShapeDtypeStruct((m, D_MODEL), F32),
        compiler_params=pltpu.CompilerParams(
            dimension_semantics=("arbitrary",), vmem_limit_bytes=48 * MIB),
        name="merge_out",
    )(yb, gb, ma, x, wpb, wo, gpost)


def kernel(x, w_in, b_gate, g_pre, g_post, sgu_ln_g, sgu_ln_b, w_spatial, b_spatial,
           w_proj_a, w_proj_b, w_out):
    batch, seq, d_model = x.shape
    depth = w_in.shape[0]
    assert d_model == D_MODEL and w_in.shape[2] == N_IN
    assert seq % TM_PROJ == 0 and (batch * seq) % TM_OUT == 0
    rows = batch * seq
    for l in range(depth):
        q4, k4, v4, zb, gb, ma = _proj_call(
            x, g_pre[l][None], w_in[l].astype(BF16), b_gate[l][None],
            sgu_ln_g[l][None], sgu_ln_b[l][None], w_spatial[l].astype(BF16),
            b_spatial[l].T, w_proj_a[l].astype(BF16))
        yb = _attn_call(q4, k4, v4, zb)
        x = _out_call(yb.reshape(rows, D_B), gb.reshape(rows, D_B), ma.reshape(rows, D_B),
                      x.reshape(rows, d_model), w_proj_b[l].astype(BF16), w_out[l].astype(BF16),
                      g_post[l][None]).reshape(batch, seq, d_model)
    return x
```

```python
import math

import jax
import jax.numpy as jnp
from jax import lax
from jax.experimental import pallas as pl
from jax.experimental.pallas import tpu as pltpu

D_MODEL = 1024
D_A = 2 * D_MODEL
CHUNK = 128
G_A = D_A // 128
N_HEADS = 8
HEAD_DIM = 128
D_B = N_HEADS * HEAD_DIM
DILATIONS = (1, 4, 16)
N_SIDE = 64
NEG = -1e30
EPS = 1e-6
LOG2E = math.log2(math.e)

OFF_U = 0
OFF_V = OFF_U + D_A
OFF_ZA = OFF_V + D_A
OFF_Q = OFF_ZA + D_A
OFF_K = OFF_Q + D_B
OFF_VB = OFF_K + D_B
OFF_ZB = OFF_VB + D_B
OFF_GA = OFF_ZB + D_B
OFF_GB = OFF_GA + D_MODEL
N_IN = OFF_GB + D_MODEL

QBLK = 128
KBLK = QBLK + 2 * N_SIDE
GROUP = 8
TM_PROJ = 256
TM_OUT = 1024
OUT_SLAB = 256
CW = 512
MIB = 1024 * 1024

F32 = jnp.float32
BF16 = jnp.bfloat16


def _gelu(x):
    c = math.sqrt(2.0 / math.pi)
    return 0.5 * x * (1.0 + jnp.tanh(c * (x + 0.044715 * (x * x * x))))


def _sigmoid(x):
    return 0.5 * (jnp.tanh(0.5 * x) + 1.0)


def _silu(x):
    return x * _sigmoid(x)


def _const_spec(shape):
    nd = len(shape)
    return pl.BlockSpec(shape, lambda *_: (0,) * nd, pipeline_mode=pl.Buffered(1))


def _proj_kernel(x_ref, gpre_ref, w_in_ref, bgate_ref, lng_ref, lnb_ref, ws_ref, bst_ref, wpa_ref,
                 q_ref, k_ref, v_ref, zb_ref, gb_ref, ma_ref,
                 t_scr, v_scr, vn_scr, ya_scr):
    tm = x_ref.shape[0]
    x = x_ref[...]
    inv = lax.rsqrt(jnp.mean(x * x, axis=-1, keepdims=True) + EPS)
    h = (x * inv * gpre_ref[...]).astype(BF16)

    def proj(off, c):
        return jnp.dot(h, w_in_ref[:, off + c:off + c + CW], preferred_element_type=F32)

    for c in range(0, D_A, CW):
        v_scr[:, c:c + CW] = _gelu(proj(OFF_V, c))

    v = v_scr[...]
    mu = jnp.mean(v, axis=-1, keepdims=True)
    dv = v - mu
    var = jnp.mean(dv * dv, axis=-1, keepdims=True)
    vn_scr[...] = (dv * lax.rsqrt(var + EPS) * lng_ref[...] + lnb_ref[...]).astype(BF16)

    scale = LOG2E * HEAD_DIM ** -0.5

    def attn_branch(c):
        cs = slice(c, c + CW)
        gb_ref[:, cs] = _sigmoid(proj(OFF_GB, c) + bgate_ref[:, D_MODEL + c:D_MODEL + c + CW]).astype(BF16)
        zb_ref[:, cs] = _silu(proj(OFF_ZB, c)).astype(BF16)
        q_ref[:, cs] = proj(OFF_Q, c) * scale
        k_ref[:, cs] = proj(OFF_K, c)
        v_ref[:, cs] = proj(OFF_VB, c)

    for c in range(0, D_A, CW):
        t_scr[:, c:c + CW] = _gelu(proj(OFF_U, c)) * _silu(proj(OFF_ZA, c))
        if c % (2 * CW) == 0:
            attn_branch(c // 2)

    assert tm % (2 * CHUNK) == 0
    for c in range(0, tm, 2 * CHUNK):
        rows0 = slice(c, c + CHUNK)
        rows1 = slice(c + CHUNK, c + 2 * CHUNK)
        for g in range(G_A):
            cols = slice(g * 128, (g + 1) * 128)
            vn2 = jnp.concatenate([vn_scr[rows0, cols], vn_scr[rows1, cols]], axis=1)
            s2 = jnp.dot(ws_ref[g], vn2, preferred_element_type=F32) + bst_ref[:, g:g + 1]
            ya_scr[rows0, cols] = (t_scr[rows0, cols] * s2[:, :128]).astype(BF16)
            ya_scr[rows1, cols] = (t_scr[rows1, cols] * s2[:, 128:]).astype(BF16)

    for c in range(0, D_MODEL, CW):
        cs = slice(c, c + CW)
        ya_p = jnp.dot(ya_scr[...], wpa_ref[:, cs], preferred_element_type=F32)
        ma_ref[:, cs] = (_sigmoid(proj(OFF_GA, c) + bgate_ref[:, cs]) * ya_p).astype(BF16)


def _proj_call(x, gpre, w_in, bgate, lng, lnb, ws, bst, wpa):
    m = x.shape[0]
    tm = TM_PROJ
    row = lambda i: (i, 0)
    out_f32 = jax.ShapeDtypeStruct((m, D_B), F32)
    out_bf16 = jax.ShapeDtypeStruct((m, D_B), BF16)
    return pl.pallas_call(
        _proj_kernel,
        grid=(m // tm,),
        in_specs=[
            pl.BlockSpec((tm, D_MODEL), row),
            _const_spec((1, D_MODEL)),
            _const_spec((D_MODEL, N_IN)),
            _const_spec((1, 2 * D_MODEL)),
            _const_spec((1, D_A)),
            _const_spec((1, D_A)),
            _const_spec((G_A, CHUNK, CHUNK)),
            _const_spec((CHUNK, G_A)),
            _const_spec((D_A, D_MODEL)),
        ],
        out_specs=[pl.BlockSpec((tm, D_B), row)] * 6,
        out_shape=[out_f32, out_f32, out_f32, out_bf16, out_bf16, out_bf16],
        scratch_shapes=[
            pltpu.VMEM((tm, D_A), F32),
            pltpu.VMEM((tm, D_A), F32),
            pltpu.VMEM((tm, D_A), BF16),
            pltpu.VMEM((tm, D_A), BF16),
        ],
        compiler_params=pltpu.CompilerParams(
            dimension_semantics=("arbitrary",), vmem_limit_bytes=56 * MIB),
        name="proj_gmlp",
    )(x, gpre, w_in, bgate, lng, lnb, ws, bst, wpa)


def _attn_kernel(q_ref, k_ref, v_ref, zb_ref, y_ref, qd, kd, vd, f4q, f4k, f4v, o_scr, d_scr, m_scr,
                 s_scr, p_scr, bias_scr):
    seq = q_ref.shape[0]
    n_pat = len(DILATIONS)
    blocks_per_pat = seq // QBLK
    head = pl.program_id(1)

    slope = LOG2E * jnp.exp2(jnp.full((QBLK, KBLK), -1.0, F32) * (head + 1).astype(F32))
    qi = lax.broadcasted_iota(jnp.int32, (QBLK, KBLK), 0)
    ki = lax.broadcasted_iota(jnp.int32, (QBLK, KBLK), 1)
    for p, d in enumerate(DILATIONS):
        for case, shift in enumerate((N_SIDE, 0, KBLK - QBLK)):
            rel = jnp.abs(ki - shift - qi)
            bias_scr[3 * p + case] = jnp.where(rel <= N_SIDE, -slope * (d * rel).astype(F32), NEG)

    assert DILATIONS == (1, 4, 16)
    rows_per_copy = 256
    n_copies = seq // rows_per_copy
    srcs = (q_ref, k_ref, v_ref)
    dsts = (qd, kd, vd)
    mids = (f4q, f4k, f4v)

    def deinterleave(i, carry):
        dst = pl.ds(pl.multiple_of(i * rows_per_copy, rows_per_copy), rows_per_copy)
        src4 = pl.ds(i // 4 + 4 * rows_per_copy * (i % 4), rows_per_copy, stride=4)
        for src, mid, out in zip(srcs, mids, dsts):
            out[0, dst, :] = src[dst, :].astype(BF16)
            x4 = src[src4, :]
            mid[dst, :] = x4
            out[1, dst, :] = x4.astype(BF16)
        return carry

    lax.fori_loop(0, n_copies, deinterleave, 0)

    def deinterleave16(i, carry):
        dst = pl.ds(pl.multiple_of(i * rows_per_copy, rows_per_copy), rows_per_copy)
        src = pl.ds((i % 4) * (seq // 4) + i // 4, rows_per_copy, stride=4)
        for mid, out in zip(mids, dsts):
            out[2, dst, :] = mid[src, :].astype(BF16)
        return carry

    assert seq // 16 == rows_per_copy
    lax.fori_loop(0, n_copies, deinterleave16, 0)

    def block_addr(g):
        p, j = divmod(g, blocks_per_pat)
        d = DILATIONS[p]
        sub_len = seq // d
        per_res = blocks_per_pat // d
        r, n = divmod(j, per_res)
        case = 1 if n == 0 else (2 if n == per_res - 1 else 0)
        q0 = j * QBLK
        k0 = r * sub_len + min(max(QBLK * n - N_SIDE, 0), sub_len - KBLK)
        if d == 16:
            out_rows = pl.ds((r % 4) * (seq // 4) + r // 4 + 4 * QBLK * n, QBLK, stride=4)
        else:
            out_rows = pl.ds(q0, QBLK)
        return p, q0, k0, 3 * p + case, out_rows

    def scores_stage(grp, slot):
        for b in range(GROUP):
            p, q0, k0, _, _ = block_addr(grp * GROUP + b)
            s_scr[slot, b * QBLK:(b + 1) * QBLK, :] = lax.dot_general(
                qd[p, q0:q0 + QBLK, :], kd[p, k0:k0 + KBLK, :],
                (((1,), (1,)), ((), ())), preferred_element_type=F32)

    def softmax_stage(grp, slot):
        for b in range(GROUP):
            p, _, _, bias_idx, out_rows = block_addr(grp * GROUP + b)
            rows = slice(b * QBLK, (b + 1) * QBLK)
            s = s_scr[slot, rows, :] + bias_scr[bias_idx]
            mx = jnp.max(s, axis=-1, keepdims=True)
            p_scr[slot, rows, :] = jnp.exp2(s - mx).astype(BF16)
            m_scr[p, out_rows, :] = jnp.broadcast_to(mx, (QBLK, HEAD_DIM))

    ones = jnp.ones((KBLK, HEAD_DIM), BF16)

    def values_stage(grp, slot):
        for b in range(GROUP):
            p, _, k0, _, out_rows = block_addr(grp * GROUP + b)
            v_aug = jnp.concatenate([vd[p, k0:k0 + KBLK, :], ones], axis=1)
            oa = jnp.dot(p_scr[slot, b * QBLK:(b + 1) * QBLK, :], v_aug, preferred_element_type=F32)
            o_scr[p, out_rows, :] = oa[:, :HEAD_DIM]
            d_scr[p, out_rows, :] = oa[:, HEAD_DIM:]

    n_groups = n_pat * blocks_per_pat // GROUP
    for t in range(n_groups + 2):
        if 0 <= t - 2:
            values_stage(t - 2, t % 2)
        if 0 <= t - 1 < n_groups:
            softmax_stage(t - 1, (t - 1) % 2)
        if t < n_groups:
            scores_stage(t, t % 2)

    rows = 64
    n_chunks = seq // rows
    y_nat = f4q

    def combine(i, carry):
        r4 = i // (seq // 4 // rows)
        m0 = (i % (seq // 4 // rows)) * rows
        s4 = pl.ds(pl.multiple_of(i * rows, rows), rows)
        s1 = pl.ds(r4 + 4 * m0, rows, stride=4)
        a1, a4, a16 = m_scr[0, s1, :], m_scr[1, s4, :], m_scr[2, s4, :]
        mx = jnp.maximum(jnp.maximum(a1, a4), a16)
        w1, w4, w16 = jnp.exp2(a1 - mx), jnp.exp2(a4 - mx), jnp.exp2(a16 - mx)
        num = w1 * o_scr[0, s1, :] + w4 * o_scr[1, s4, :] + w16 * o_scr[2, s4, :]
        den = w1 * d_scr[0, s1, :] + w4 * d_scr[1, s4, :] + w16 * d_scr[2, s4, :]
        y_nat[s1, :] = num / den
        return carry

    lax.fori_loop(0, n_chunks, combine, 0, unroll=2)
    y_ref[...] = (y_nat[...] * zb_ref[...].astype(F32)).astype(BF16)


def _attn_call(q, k, v, zb, batch, seq):
    n_pat = len(DILATIONS)
    blk = pl.BlockSpec((seq, HEAD_DIM), lambda b, h: (b, h))
    pat_f32 = pltpu.VMEM((n_pat, seq, HEAD_DIM), F32)
    pat_bf16 = pltpu.VMEM((n_pat, seq, HEAD_DIM), BF16)
    return pl.pallas_call(
        _attn_kernel,
        grid=(batch, N_HEADS),
        in_specs=[blk, blk, blk, blk],
        out_specs=blk,
        out_shape=jax.ShapeDtypeStruct((batch * seq, D_B), BF16),
        scratch_shapes=[pat_bf16] * 3 + [pltpu.VMEM((seq, HEAD_DIM), F32)] * 3 + [pat_f32] * 3
        + [pltpu.VMEM((2, GROUP * QBLK, KBLK), F32),
           pltpu.VMEM((2, GROUP * QBLK, KBLK), BF16),
           pltpu.VMEM((3 * n_pat, QBLK, KBLK), F32)],
        compiler_params=pltpu.CompilerParams(
            dimension_semantics=("arbitrary", "arbitrary"), vmem_limit_bytes=58 * MIB),
        name="dilated_attn",
    )(q, k, v, zb)


def _out_kernel(yb_ref, gb_ref, ma_ref, x_ref, wpb_ref, wo_ref, gpost_ref, o_ref):
    tm = o_ref.shape[0]
    for r0 in range(0, tm, OUT_SLAB):
        rows = slice(r0, r0 + OUT_SLAB)
        mb = jnp.dot(yb_ref[rows, :], wpb_ref[...], preferred_element_type=F32)
        m = ma_ref[rows, :].astype(F32) + gb_ref[rows, :].astype(F32) * mb
        r = jnp.dot(m.astype(BF16), wo_ref[...], preferred_element_type=F32)
        inv = lax.rsqrt(jnp.mean(r * r, axis=-1, keepdims=True) + EPS)
        o_ref[rows, :] = x_ref[rows, :] + r * inv * gpost_ref[...]


def _out_call(yb, gb, ma, x, wpb, wo, gpost):
    m = x.shape[0]
    tm = TM_OUT
    row = pl.BlockSpec((tm, D_MODEL), lambda i: (i, 0))
    return pl.pallas_call(
        _out_kernel,
        grid=(m // tm,),
        in_specs=[row, row, row, row,
                  _const_spec((D_B, D_MODEL)), _const_spec((D_MODEL, D_MODEL)),
                  _const_spec((1, D_MODEL))],
        out_specs=row,
        out_shape=jax.ShapeDtypeStruct((m, D_MODEL), F32),
        compiler_params=pltpu.CompilerParams(
            dimension_semantics=("arbitrary",), vmem_limit_bytes=48 * MIB),
        name="merge_out",
    )(yb, gb, ma, x, wpb, wo, gpost)


def kernel(x, w_in, b_gate, g_pre, g_post, sgu_ln_g, sgu_ln_b, w_spatial, b_spatial,
           w_proj_a, w_proj_b, w_out):
    batch, seq, d_model = x.shape
    depth = w_in.shape[0]
    assert d_model == D_MODEL and w_in.shape[2] == N_IN
    assert seq % (16 * 256) == 0 and (batch * seq) % TM_OUT == 0
    xf = x.reshape(batch * seq, d_model)
    for l in range(depth):
        q, k, v, zb, gb, ma = _proj_call(
            xf, g_pre[l][None], w_in[l].astype(BF16), b_gate[l][None],
            sgu_ln_g[l][None], sgu_ln_b[l][None], w_spatial[l].astype(BF16),
            b_spatial[l].T, w_proj_a[l].astype(BF16))
        yb = _attn_call(q, k, v, zb, batch, seq)
        xf = _out_call(yb, gb, ma, xf, w_proj_b[l].astype(BF16), w_out[l].astype(BF16),
                       g_post[l][None])
    return xf.reshape(batch, seq, d_model)
```

```python
import math

import jax
import jax.numpy as jnp
from jax import lax
from jax.experimental import pallas as pl
from jax.experimental.pallas import tpu as pltpu

D_MODEL = 1024
D_A = 2 * D_MODEL
CHUNK = 128
G_A = D_A // 128
N_HEADS = 8
HEAD_DIM = 128
D_B = N_HEADS * HEAD_DIM
DILATIONS = (1, 4, 16)
N_SIDE = 64
NEG = -1e30
EPS = 1e-6
LOG2E = math.log2(math.e)

OFF_U = 0
OFF_V = OFF_U + D_A
OFF_ZA = OFF_V + D_A
OFF_Q = OFF_ZA + D_A
OFF_K = OFF_Q + D_B
OFF_VB = OFF_K + D_B
OFF_ZB = OFF_VB + D_B
OFF_GA = OFF_ZB + D_B
OFF_GB = OFF_GA + D_MODEL
N_IN = OFF_GB + D_MODEL

QBLK = 128
KBLK = QBLK + 2 * N_SIDE
GROUP = 8
MIX_ROWS = 32
LAYOUT_ITERS = 4
TM_PROJ = 256
TM_OUT = 1024
OUT_SLAB = 256
CW = 512
MIB = 1024 * 1024

F32 = jnp.float32
BF16 = jnp.bfloat16


def _gelu(x):
    c = math.sqrt(2.0 / math.pi)
    return 0.5 * x * (1.0 + jnp.tanh(c * (x + 0.044715 * (x * x * x))))


def _sigmoid(x):
    return 0.5 * (jnp.tanh(0.5 * x) + 1.0)


def _silu(x):
    return x * _sigmoid(x)


def _const_spec(shape):
    nd = len(shape)
    return pl.BlockSpec(shape, lambda *_: (0,) * nd, pipeline_mode=pl.Buffered(1))


def _proj_kernel(x_ref, gpre_ref, w_in_ref, bgate_ref, lng_ref, lnb_ref, ws_ref, bst_ref, wpa_ref,
                 q_ref, k_ref, v_ref, zb_ref, gb_ref, ma_ref,
                 t_scr, v_scr, vn_scr, ya_scr):
    tm = x_ref.shape[0]
    x = x_ref[...]
    inv = lax.rsqrt(jnp.mean(x * x, axis=-1, keepdims=True) + EPS)
    h = (x * inv * gpre_ref[...]).astype(BF16)

    def proj(off, c):
        return jnp.dot(h, w_in_ref[:, off + c:off + c + CW], preferred_element_type=F32)

    for c in range(0, D_A, CW):
        v_scr[:, c:c + CW] = _gelu(proj(OFF_V, c))

    v = v_scr[...]
    mu = jnp.mean(v, axis=-1, keepdims=True)
    dv = v - mu
    var = jnp.mean(dv * dv, axis=-1, keepdims=True)
    vn_scr[...] = (dv * lax.rsqrt(var + EPS) * lng_ref[...] + lnb_ref[...]).astype(BF16)

    scale = LOG2E * HEAD_DIM ** -0.5

    def attn_branch(c):
        cs = slice(c, c + CW)
        gb_ref[:, cs] = _sigmoid(proj(OFF_GB, c) + bgate_ref[:, D_MODEL + c:D_MODEL + c + CW]).astype(BF16)
        zb_ref[:, cs] = _silu(proj(OFF_ZB, c)).astype(BF16)
        q_ref[:, cs] = proj(OFF_Q, c) * scale
        k_ref[:, cs] = proj(OFF_K, c)
        v_ref[:, cs] = proj(OFF_VB, c)

    for c in range(0, D_A, CW):
        t_scr[:, c:c + CW] = _gelu(proj(OFF_U, c)) * _silu(proj(OFF_ZA, c))
        if c % (2 * CW) == 0:
            attn_branch(c // 2)

    assert tm % (2 * CHUNK) == 0
    for c in range(0, tm, 2 * CHUNK):
        rows0 = slice(c, c + CHUNK)
        rows1 = slice(c + CHUNK, c + 2 * CHUNK)
        for g in range(G_A):
            cols = slice(g * 128, (g + 1) * 128)
            vn2 = jnp.concatenate([vn_scr[rows0, cols], vn_scr[rows1, cols]], axis=1)
            s2 = jnp.dot(ws_ref[g], vn2, preferred_element_type=F32) + bst_ref[:, g:g + 1]
            ya_scr[rows0, cols] = (t_scr[rows0, cols] * s2[:, :128]).astype(BF16)
            ya_scr[rows1, cols] = (t_scr[rows1, cols] * s2[:, 128:]).astype(BF16)

    for c in range(0, D_MODEL, CW):
        cs = slice(c, c + CW)
        ya_p = jnp.dot(ya_scr[...], wpa_ref[:, cs], preferred_element_type=F32)
        ma_ref[:, cs] = (_sigmoid(proj(OFF_GA, c) + bgate_ref[:, cs]) * ya_p).astype(BF16)


def _proj_call(x, gpre, w_in, bgate, lng, lnb, ws, bst, wpa):
    m = x.shape[0]
    tm = TM_PROJ
    row = lambda i: (i, 0)
    out_f32 = jax.ShapeDtypeStruct((m, D_B), F32)
    out_bf16 = jax.ShapeDtypeStruct((m, D_B), BF16)
    return pl.pallas_call(
        _proj_kernel,
        grid=(m // tm,),
        in_specs=[
            pl.BlockSpec((tm, D_MODEL), row),
            _const_spec((1, D_MODEL)),
            _const_spec((D_MODEL, N_IN)),
            _const_spec((1, 2 * D_MODEL)),
            _const_spec((1, D_A)),
            _const_spec((1, D_A)),
            _const_spec((G_A, CHUNK, CHUNK)),
            _const_spec((CHUNK, G_A)),
            _const_spec((D_A, D_MODEL)),
        ],
        out_specs=[pl.BlockSpec((tm, D_B), row)] * 6,
        out_shape=[out_f32, out_f32, out_f32, out_bf16, out_bf16, out_bf16],
        scratch_shapes=[
            pltpu.VMEM((tm, D_A), F32),
            pltpu.VMEM((tm, D_A), F32),
            pltpu.VMEM((tm, D_A), BF16),
            pltpu.VMEM((tm, D_A), BF16),
        ],
        compiler_params=pltpu.CompilerParams(
            dimension_semantics=("arbitrary",), vmem_limit_bytes=56 * MIB),
        name="proj_gmlp",
    )(x, gpre, w_in, bgate, lng, lnb, ws, bst, wpa)


def _attn_kernel(q_ref, k_ref, v_ref, zb_ref, y_ref, qd, kd, vd, f4q, f4k, f4v, o_scr, d_scr, m_scr,
                 s_scr, p_scr, bias_scr):
    seq = q_ref.shape[0]
    n_pat = len(DILATIONS)
    blocks_per_pat = seq // QBLK
    head = pl.program_id(1)

    slope = LOG2E * jnp.exp2(jnp.full((QBLK, KBLK), -1.0, F32) * (head + 1).astype(F32))
    qi = lax.broadcasted_iota(jnp.int32, (QBLK, KBLK), 0)
    ki = lax.broadcasted_iota(jnp.int32, (QBLK, KBLK), 1)
    for p, d in enumerate(DILATIONS):
        for case, shift in enumerate((N_SIDE, 0, KBLK - QBLK)):
            rel = jnp.abs(ki - shift - qi)
            bias_scr[3 * p + case] = jnp.where(rel <= N_SIDE, -slope * (d * rel).astype(F32), NEG)

    assert DILATIONS == (1, 4, 16)
    rows_per_copy = 256
    n_copies = seq // rows_per_copy
    srcs = (q_ref, k_ref, v_ref)
    dsts = (qd, kd, vd)
    mids = (f4q, f4k, f4v)

    def cast_natural(i):
        dst = slice(i * rows_per_copy, (i + 1) * rows_per_copy)
        for src, out in zip(srcs, dsts):
            out[0, dst, :] = src[dst, :].astype(BF16)

    def deinterleave4(i):
        dst = slice(i * rows_per_copy, (i + 1) * rows_per_copy)
        src4 = pl.ds(i // 4 + 4 * rows_per_copy * (i % 4), rows_per_copy, stride=4)
        for src, mid, out in zip(srcs, mids, dsts):
            x4 = src[src4, :]
            mid[dst, :] = x4
            out[1, dst, :] = x4.astype(BF16)

    def deinterleave16(i):
        dst = slice(i * rows_per_copy, (i + 1) * rows_per_copy)
        src = pl.ds((i % 4) * (seq // 4) + i // 4, rows_per_copy, stride=4)
        for mid, out in zip(mids, dsts):
            out[2, dst, :] = mid[src, :].astype(BF16)

    assert seq // 16 == rows_per_copy
    for i in range(n_copies):
        cast_natural(i)
    layout_work = [(deinterleave4, i) for i in range(n_copies)] + [(deinterleave16, i) for i in range(n_copies)]

    schedule = [(p, j) for p in (0, 2, 1) for j in range(blocks_per_pat)]
    assert sorted(schedule) == [(p, j) for p in range(n_pat) for j in range(blocks_per_pat)]
    od_slab = {0: 0, 2: 1}
    y_nat = f4q

    def block_addr(g):
        p, j = schedule[g]
        d = DILATIONS[p]
        sub_len = seq // d
        per_res = blocks_per_pat // d
        r, n = divmod(j, per_res)
        case = 1 if n == 0 else (2 if n == per_res - 1 else 0)
        q0 = j * QBLK
        k0 = r * sub_len + min(max(QBLK * n - N_SIDE, 0), sub_len - KBLK)
        if d == 16:
            out_rows = pl.ds((r % 4) * (seq // 4) + r // 4 + 4 * QBLK * n, QBLK, stride=4)
        else:
            out_rows = pl.ds(q0, QBLK)
        nat_rows = r + 4 * QBLK * n if d == 4 else None
        return p, q0, k0, 3 * p + case, out_rows, nat_rows

    def scores_stage(grp, slot):
        for b in range(GROUP):
            p, q0, k0, _, _, _ = block_addr(grp * GROUP + b)
            s_scr[slot, b * QBLK:(b + 1) * QBLK, :] = lax.dot_general(
                qd[p, q0:q0 + QBLK, :], kd[p, k0:k0 + KBLK, :],
                (((1,), (1,)), ((), ())), preferred_element_type=F32)

    def softmax_stage(grp, slot):
        for b in range(GROUP):
            p, _, _, bias_idx, out_rows, _ = block_addr(grp * GROUP + b)
            rows = slice(b * QBLK, (b + 1) * QBLK)
            s = s_scr[slot, rows, :] + bias_scr[bias_idx]
            mx = jnp.max(s, axis=-1, keepdims=True)
            p_scr[slot, rows, :] = jnp.exp2(s - mx).astype(BF16)
            m_scr[p, out_rows, :] = jnp.broadcast_to(mx, (QBLK, HEAD_DIM))

    ones = jnp.ones((KBLK, HEAD_DIM), BF16)

    def values_stage(grp, slot):
        for b in range(GROUP):
            p, q0, k0, _, out_rows, nat0 = block_addr(grp * GROUP + b)
            v_aug = jnp.concatenate([vd[p, k0:k0 + KBLK, :], ones], axis=1)
            oa = jnp.dot(p_scr[slot, b * QBLK:(b + 1) * QBLK, :], v_aug, preferred_element_type=F32)
            if nat0 is None:
                o_scr[od_slab[p], out_rows, :] = oa[:, :HEAD_DIM]
                d_scr[od_slab[p], out_rows, :] = oa[:, HEAD_DIM:]
                continue
            for c in range(0, QBLK, MIX_ROWS):
                by4 = pl.ds(q0 + c, MIX_ROWS)
                nat = pl.ds(nat0 + 4 * c, MIX_ROWS, stride=4)
                a1, a4, a16 = m_scr[0, nat, :], m_scr[1, by4, :], m_scr[2, by4, :]
                mx = jnp.maximum(jnp.maximum(a1, a4), a16)
                w1, w4, w16 = jnp.exp2(a1 - mx), jnp.exp2(a4 - mx), jnp.exp2(a16 - mx)
                o4, den4 = oa[c:c + MIX_ROWS, :HEAD_DIM], oa[c:c + MIX_ROWS, HEAD_DIM:]
                num = w1 * o_scr[0, nat, :] + w4 * o4 + w16 * o_scr[1, by4, :]
                den = w1 * d_scr[0, nat, :] + w4 * den4 + w16 * d_scr[1, by4, :]
                y_nat[nat, :] = num / den

    n_groups = n_pat * blocks_per_pat // GROUP
    per_iter = -(-len(layout_work) // LAYOUT_ITERS)
    for t in range(n_groups + 2):
        for fn, i in layout_work[t * per_iter:(t + 1) * per_iter]:
            fn(i)
        if 0 <= t - 2:
            values_stage(t - 2, t % 2)
        if 0 <= t - 1 < n_groups:
            softmax_stage(t - 1, (t - 1) % 2)
        if t < n_groups:
            scores_stage(t, t % 2)

    y_ref[...] = (y_nat[...] * zb_ref[...].astype(F32)).astype(BF16)


def _attn_call(q, k, v, zb, batch, seq):
    n_pat = len(DILATIONS)
    blk = pl.BlockSpec((seq, HEAD_DIM), lambda b, h: (b, h))
    pat_f32 = pltpu.VMEM((n_pat, seq, HEAD_DIM), F32)
    pat_bf16 = pltpu.VMEM((n_pat, seq, HEAD_DIM), BF16)
    return pl.pallas_call(
        _attn_kernel,
        grid=(batch, N_HEADS),
        in_specs=[blk, blk, blk, blk],
        out_specs=blk,
        out_shape=jax.ShapeDtypeStruct((batch * seq, D_B), BF16),
        scratch_shapes=[pat_bf16] * 3 + [pltpu.VMEM((seq, HEAD_DIM), F32)] * 3
        + [pltpu.VMEM((2, seq, HEAD_DIM), F32)] * 2 + [pat_f32]
        + [pltpu.VMEM((2, GROUP * QBLK, KBLK), F32),
           pltpu.VMEM((2, GROUP * QBLK, KBLK), BF16),
           pltpu.VMEM((3 * n_pat, QBLK, KBLK), F32)],
        compiler_params=pltpu.CompilerParams(
            dimension_semantics=("arbitrary", "arbitrary"), vmem_limit_bytes=58 * MIB),
        name="dilated_attn",
    )(q, k, v, zb)


def _out_kernel(yb_ref, gb_ref, ma_ref, x_ref, wpb_ref, wo_ref, gpost_ref, o_ref):
    tm = o_ref.shape[0]
    for r0 in range(0, tm, OUT_SLAB):
        rows = slice(r0, r0 + OUT_SLAB)
        mb = jnp.dot(yb_ref[rows, :], wpb_ref[...], preferred_element_type=F32)
        m = ma_ref[rows, :].astype(F32) + gb_ref[rows, :].astype(F32) * mb
        r = jnp.dot(m.astype(BF16), wo_ref[...], preferred_element_type=F32)
        inv = lax.rsqrt(jnp.mean(r * r, axis=-1, keepdims=True) + EPS)
        o_ref[rows, :] = x_ref[rows, :] + r * inv * gpost_ref[...]


def _out_call(yb, gb, ma, x, wpb, wo, gpost):
    m = x.shape[0]
    tm = TM_OUT
    row = pl.BlockSpec((tm, D_MODEL), lambda i: (i, 0))
    return pl.pallas_call(
        _out_kernel,
        grid=(m // tm,),
        in_specs=[row, row, row, row,
                  _const_spec((D_B, D_MODEL)), _const_spec((D_MODEL, D_MODEL)),
                  _const_spec((1, D_MODEL))],
        out_specs=row,
        out_shape=jax.ShapeDtypeStruct((m, D_MODEL), F32),
        compiler_params=pltpu.CompilerParams(
            dimension_semantics=("arbitrary",), vmem_limit_bytes=48 * MIB),
        name="merge_out",
    )(yb, gb, ma, x, wpb, wo, gpost)


def kernel(x, w_in, b_gate, g_pre, g_post, sgu_ln_g, sgu_ln_b, w_spatial, b_spatial,
           w_proj_a, w_proj_b, w_out):
    batch, seq, d_model = x.shape
    depth = w_in.shape[0]
    assert d_model == D_MODEL and w_in.shape[2] == N_IN
    assert seq % (16 * 256) == 0 and (batch * seq) % TM_OUT == 0
    xf = x.reshape(batch * seq, d_model)
    for l in range(depth):
        q, k, v, zb, gb, ma = _proj_call(
            xf, g_pre[l][None], w_in[l].astype(BF16), b_gate[l][None],
            sgu_ln_g[l][None], sgu_ln_b[l][None], w_spatial[l].astype(BF16),
            b_spatial[l].T, w_proj_a[l].astype(BF16))
        yb = _attn_call(q, k, v, zb, batch, seq)
        xf = _out_call(yb, gb, ma, xf, w_proj_b[l].astype(BF16), w_out[l].astype(BF16),
                       g_post[l][None])
    return xf.reshape(batch, seq, d_model)
```

```python
import math

import jax
import jax.numpy as jnp
from jax import lax
from jax.experimental import pallas as pl
from jax.experimental.pallas import tpu as pltpu

D_MODEL = 1024
D_A = 2 * D_MODEL
CHUNK = 128
G_A = D_A // 128
N_HEADS = 8
HEAD_DIM = 128
D_B = N_HEADS * HEAD_DIM
DILATIONS = (1, 4, 16)
N_SIDE = 64
NEG = -1e30
EPS = 1e-6
LOG2E = math.log2(math.e)

OFF_U = 0
OFF_V = OFF_U + D_A
OFF_ZA = OFF_V + D_A
OFF_Q = OFF_ZA + D_A
OFF_K = OFF_Q + D_B
OFF_VB = OFF_K + D_B
OFF_ZB = OFF_VB + D_B
OFF_GA = OFF_ZB + D_B
OFF_GB = OFF_GA + D_MODEL
N_IN = OFF_GB + D_MODEL

QBLK = 128
KBLK = QBLK + 2 * N_SIDE
GROUP = 8
MIX_ROWS = 32
LAYOUT_ITERS = 4
TM_PROJ = 512
TM_OUT = 1024
OUT_SLAB = 256
CW = 512
MIB = 1024 * 1024

F32 = jnp.float32
BF16 = jnp.bfloat16


def _gelu(x):
    c = math.sqrt(2.0 / math.pi)
    return 0.5 * x * (1.0 + jnp.tanh(c * (x + 0.044715 * (x * x * x))))


def _sigmoid(x):
    return 0.5 * (jnp.tanh(0.5 * x) + 1.0)


def _silu(x):
    return x * _sigmoid(x)


def _const_spec(shape):
    nd = len(shape)
    return pl.BlockSpec(shape, lambda *_: (0,) * nd, pipeline_mode=pl.Buffered(1))


def _normed_input(x_ref, gpre_ref):
    x = x_ref[...]
    inv = lax.rsqrt(jnp.mean(x * x, axis=-1, keepdims=True) + EPS)
    return (x * inv * gpre_ref[...]).astype(BF16)


def _gmlp_kernel(x_ref, gpre_ref, w_uvz_ref, w_ga_ref, bga_ref, lng_ref, lnb_ref, ws_ref, bst_ref, wpa_ref,
                 ma_ref, t_scr, v_scr, vn_scr, ya_scr):
    tm = x_ref.shape[0]
    h = _normed_input(x_ref, gpre_ref)

    def proj(off, c):
        return jnp.dot(h, w_uvz_ref[:, off + c:off + c + CW], preferred_element_type=F32)

    for c in range(0, D_A, CW):
        v_scr[:, c:c + CW] = _gelu(proj(OFF_V, c))

    v = v_scr[...]
    mu = jnp.mean(v, axis=-1, keepdims=True)
    dv = v - mu
    var = jnp.mean(dv * dv, axis=-1, keepdims=True)
    vn_scr[...] = (dv * lax.rsqrt(var + EPS) * lng_ref[...] + lnb_ref[...]).astype(BF16)

    for c in range(0, D_A, CW):
        t_scr[:, c:c + CW] = _gelu(proj(OFF_U, c)) * _silu(proj(OFF_ZA, c))

    assert tm % (2 * CHUNK) == 0
    for c in range(0, tm, 2 * CHUNK):
        rows0 = slice(c, c + CHUNK)
        rows1 = slice(c + CHUNK, c + 2 * CHUNK)
        for g in range(G_A):
            cols = slice(g * 128, (g + 1) * 128)
            vn2 = jnp.concatenate([vn_scr[rows0, cols], vn_scr[rows1, cols]], axis=1)
            s2 = jnp.dot(ws_ref[g], vn2, preferred_element_type=F32) + bst_ref[:, g:g + 1]
            ya_scr[rows0, cols] = (t_scr[rows0, cols] * s2[:, :128]).astype(BF16)
            ya_scr[rows1, cols] = (t_scr[rows1, cols] * s2[:, 128:]).astype(BF16)

    for c in range(0, D_MODEL, CW):
        cs = slice(c, c + CW)
        ya_p = jnp.dot(ya_scr[...], wpa_ref[:, cs], preferred_element_type=F32)
        ga = jnp.dot(h, w_ga_ref[:, cs], preferred_element_type=F32) + bga_ref[:, cs]
        ma_ref[:, cs] = (_sigmoid(ga) * ya_p).astype(BF16)


def _qkv_kernel(x_ref, gpre_ref, w_qk_ref, w_vz_ref, w_gb_ref, bgb_ref,
                q_ref, k_ref, v_ref, zb_ref, gb_ref):
    h = _normed_input(x_ref, gpre_ref)
    scale = LOG2E * HEAD_DIM ** -0.5
    for c in range(0, D_B, CW):
        cs = slice(c, c + CW)
        q_ref[:, cs] = jnp.dot(h, w_qk_ref[:, cs], preferred_element_type=F32) * scale
        k_ref[:, cs] = jnp.dot(h, w_qk_ref[:, D_B + c:D_B + c + CW], preferred_element_type=F32)
        v_ref[:, cs] = jnp.dot(h, w_vz_ref[:, cs], preferred_element_type=F32)
        zb_ref[:, cs] = _silu(jnp.dot(h, w_vz_ref[:, D_B + c:D_B + c + CW],
                                      preferred_element_type=F32)).astype(BF16)
        gb = jnp.dot(h, w_gb_ref[:, cs], preferred_element_type=F32) + bgb_ref[:, cs]
        gb_ref[:, cs] = _sigmoid(gb).astype(BF16)


def _col_block_spec(width, offset):
    assert offset % width == 0
    return pl.BlockSpec((D_MODEL, width), lambda i: (0, offset // width), pipeline_mode=pl.Buffered(1))


def _gmlp_call(x, gpre, w_in, bgate, lng, lnb, ws, bst, wpa):
    m = x.shape[0]
    tm = TM_PROJ
    assert (OFF_U, OFF_V, OFF_ZA) == (0, D_A, 2 * D_A)
    return pl.pallas_call(
        _gmlp_kernel,
        grid=(m // tm,),
        in_specs=[
            pl.BlockSpec((tm, D_MODEL), lambda i: (i, 0)),
            _const_spec((1, D_MODEL)),
            _col_block_spec(3 * D_A, OFF_U),
            _col_block_spec(D_MODEL, OFF_GA),
            pl.BlockSpec((1, D_MODEL), lambda i: (0, 0), pipeline_mode=pl.Buffered(1)),
            _const_spec((1, D_A)),
            _const_spec((1, D_A)),
            _const_spec((G_A, CHUNK, CHUNK)),
            _const_spec((CHUNK, G_A)),
            _const_spec((D_A, D_MODEL)),
        ],
        out_specs=pl.BlockSpec((tm, D_MODEL), lambda i: (i, 0)),
        out_shape=jax.ShapeDtypeStruct((m, D_MODEL), BF16),
        scratch_shapes=[
            pltpu.VMEM((tm, D_A), F32),
            pltpu.VMEM((tm, D_A), F32),
            pltpu.VMEM((tm, D_A), BF16),
            pltpu.VMEM((tm, D_A), BF16),
        ],
        compiler_params=pltpu.CompilerParams(
            dimension_semantics=("arbitrary",), vmem_limit_bytes=56 * MIB),
        name="proj_gmlp",
    )(x, gpre, w_in, w_in, bgate, lng, lnb, ws, bst, wpa)


def _qkv_call(x, gpre, w_in, bgate):
    m = x.shape[0]
    tm = TM_PROJ
    assert OFF_K == OFF_Q + D_B and OFF_ZB == OFF_VB + D_B
    row = pl.BlockSpec((tm, D_B), lambda i: (i, 0))
    out_f32 = jax.ShapeDtypeStruct((m, D_B), F32)
    out_bf16 = jax.ShapeDtypeStruct((m, D_B), BF16)
    return pl.pallas_call(
        _qkv_kernel,
        grid=(m // tm,),
        in_specs=[
            pl.BlockSpec((tm, D_MODEL), lambda i: (i, 0)),
            _const_spec((1, D_MODEL)),
            _col_block_spec(2 * D_B, OFF_Q),
            _col_block_spec(2 * D_B, OFF_VB),
            _col_block_spec(D_MODEL, OFF_GB),
            pl.BlockSpec((1, D_MODEL), lambda i: (0, 1), pipeline_mode=pl.Buffered(1)),
        ],
        out_specs=[row] * 5,
        out_shape=[out_f32, out_f32, out_f32, out_bf16, out_bf16],
        compiler_params=pltpu.CompilerParams(
            dimension_semantics=("arbitrary",), vmem_limit_bytes=48 * MIB),
        name="proj_qkv",
    )(x, gpre, w_in, w_in, w_in, bgate)


def _attn_kernel(q_ref, k_ref, v_ref, zb_ref, y_ref, qd, kd, vd, f4q, f4k, f4v, o_scr, d_scr, m_scr,
                 s_scr, p_scr, bias_scr):
    seq = q_ref.shape[0]
    n_pat = len(DILATIONS)
    blocks_per_pat = seq // QBLK
    head = pl.program_id(1)

    slope = LOG2E * jnp.exp2(jnp.full((QBLK, KBLK), -1.0, F32) * (head + 1).astype(F32))
    qi = lax.broadcasted_iota(jnp.int32, (QBLK, KBLK), 0)
    ki = lax.broadcasted_iota(jnp.int32, (QBLK, KBLK), 1)
    for p, d in enumerate(DILATIONS):
        for case, shift in enumerate((N_SIDE, 0, KBLK - QBLK)):
            rel = jnp.abs(ki - shift - qi)
            bias_scr[3 * p + case] = jnp.where(rel <= N_SIDE, -slope * (d * rel).astype(F32), NEG)

    assert DILATIONS == (1, 4, 16)
    rows_per_copy = 256
    n_copies = seq // rows_per_copy
    srcs = (q_ref, k_ref, v_ref)
    dsts = (qd, kd, vd)
    mids = (f4q, f4k, f4v)

    def cast_natural(i):
        dst = slice(i * rows_per_copy, (i + 1) * rows_per_copy)
        for src, out in zip(srcs, dsts):
            out[0, dst, :] = src[dst, :].astype(BF16)

    def deinterleave4(i):
        dst = slice(i * rows_per_copy, (i + 1) * rows_per_copy)
        src4 = pl.ds(i // 4 + 4 * rows_per_copy * (i % 4), rows_per_copy, stride=4)
        for src, mid, out in zip(srcs, mids, dsts):
            x4 = src[src4, :]
            mid[dst, :] = x4
            out[1, dst, :] = x4.astype(BF16)

    def deinterleave16(i):
        dst = slice(i * rows_per_copy, (i + 1) * rows_per_copy)
        src = pl.ds((i % 4) * (seq // 4) + i // 4, rows_per_copy, stride=4)
        for mid, out in zip(mids, dsts):
            out[2, dst, :] = mid[src, :].astype(BF16)

    assert seq // 16 == rows_per_copy
    for i in range(n_copies):
        cast_natural(i)
    layout_work = [(deinterleave4, i) for i in range(n_copies)] + [(deinterleave16, i) for i in range(n_copies)]

    schedule = [(p, j) for p in (0, 2, 1) for j in range(blocks_per_pat)]
    assert sorted(schedule) == [(p, j) for p in range(n_pat) for j in range(blocks_per_pat)]
    od_slab = {0: 0, 2: 1}
    y_nat = f4q

    def block_addr(g):
        p, j = schedule[g]
        d = DILATIONS[p]
        sub_len = seq // d
        per_res = blocks_per_pat // d
        r, n = divmod(j, per_res)
        case = 1 if n == 0 else (2 if n == per_res - 1 else 0)
        q0 = j * QBLK
        k0 = r * sub_len + min(max(QBLK * n - N_SIDE, 0), sub_len - KBLK)
        if d == 16:
            out_rows = pl.ds((r % 4) * (seq // 4) + r // 4 + 4 * QBLK * n, QBLK, stride=4)
        else:
            out_rows = pl.ds(q0, QBLK)
        nat_rows = r + 4 * QBLK * n if d == 4 else None
        return p, q0, k0, 3 * p + case, out_rows, nat_rows

    def scores_stage(grp, slot):
        for b in range(GROUP):
            p, q0, k0, _, _, _ = block_addr(grp * GROUP + b)
            s_scr[slot, b * QBLK:(b + 1) * QBLK, :] = lax.dot_general(
                qd[p, q0:q0 + QBLK, :], kd[p, k0:k0 + KBLK, :],
                (((1,), (1,)), ((), ())), preferred_element_type=F32)

    def softmax_stage(grp, slot):
        for b in range(GROUP):
            p, _, _, bias_idx, out_rows, _ = block_addr(grp * GROUP + b)
            rows = slice(b * QBLK, (b + 1) * QBLK)
            s = s_scr[slot, rows, :] + bias_scr[bias_idx]
            mx = jnp.max(s, axis=-1, keepdims=True)
            p_scr[slot, rows, :] = jnp.exp2(s - mx).astype(BF16)
            m_scr[p, out_rows, :] = jnp.broadcast_to(mx, (QBLK, HEAD_DIM))

    ones = jnp.ones((KBLK, HEAD_DIM), BF16)

    def values_stage(grp, slot):
        for b in range(GROUP):
            p, q0, k0, _, out_rows, nat0 = block_addr(grp * GROUP + b)
            v_aug = jnp.concatenate([vd[p, k0:k0 + KBLK, :], ones], axis=1)
            oa = jnp.dot(p_scr[slot, b * QBLK:(b + 1) * QBLK, :], v_aug, preferred_element_type=F32)
            if nat0 is None:
                o_scr[od_slab[p], out_rows, :] = oa[:, :HEAD_DIM]
                d_scr[od_slab[p], out_rows, :] = oa[:, HEAD_DIM:]
                continue
            for c in range(0, QBLK, MIX_ROWS):
                by4 = pl.ds(q0 + c, MIX_ROWS)
                nat = pl.ds(nat0 + 4 * c, MIX_ROWS, stride=4)
                a1, a4, a16 = m_scr[0, nat, :], m_scr[1, by4, :], m_scr[2, by4, :]
                mx = jnp.maximum(jnp.maximum(a1, a4), a16)
                w1, w4, w16 = jnp.exp2(a1 - mx), jnp.exp2(a4 - mx), jnp.exp2(a16 - mx)
                o4, den4 = oa[c:c + MIX_ROWS, :HEAD_DIM], oa[c:c + MIX_ROWS, HEAD_DIM:]
                num = w1 * o_scr[0, nat, :] + w4 * o4 + w16 * o_scr[1, by4, :]
                den = w1 * d_scr[0, nat, :] + w4 * den4 + w16 * d_scr[1, by4, :]
                y_nat[nat, :] = num / den

    n_groups = n_pat * blocks_per_pat // GROUP
    per_iter = -(-len(layout_work) // LAYOUT_ITERS)
    for t in range(n_groups + 2):
        for fn, i in layout_work[t * per_iter:(t + 1) * per_iter]:
            fn(i)
        if 0 <= t - 2:
            values_stage(t - 2, t % 2)
        if 0 <= t - 1 < n_groups:
            softmax_stage(t - 1, (t - 1) % 2)
        if t < n_groups:
            scores_stage(t, t % 2)

    y_ref[...] = (y_nat[...] * zb_ref[...].astype(F32)).astype(BF16)


def _attn_call(q, k, v, zb, batch, seq):
    n_pat = len(DILATIONS)
    blk = pl.BlockSpec((seq, HEAD_DIM), lambda b, h: (b, h))
    pat_f32 = pltpu.VMEM((n_pat, seq, HEAD_DIM), F32)
    pat_bf16 = pltpu.VMEM((n_pat, seq, HEAD_DIM), BF16)
    return pl.pallas_call(
        _attn_kernel,
        grid=(batch, N_HEADS),
        in_specs=[blk, blk, blk, blk],
        out_specs=blk,
        out_shape=jax.ShapeDtypeStruct((batch * seq, D_B), BF16),
        scratch_shapes=[pat_bf16] * 3 + [pltpu.VMEM((seq, HEAD_DIM), F32)] * 3
        + [pltpu.VMEM((2, seq, HEAD_DIM), F32)] * 2 + [pat_f32]
        + [pltpu.VMEM((2, GROUP * QBLK, KBLK), F32),
           pltpu.VMEM((2, GROUP * QBLK, KBLK), BF16),
           pltpu.VMEM((3 * n_pat, QBLK, KBLK), F32)],
        compiler_params=pltpu.CompilerParams(
            dimension_semantics=("arbitrary", "arbitrary"), vmem_limit_bytes=58 * MIB),
        name="dilated_attn",
    )(q, k, v, zb)


def _out_kernel(yb_ref, gb_ref, ma_ref, x_ref, wpb_ref, wo_ref, gpost_ref, o_ref):
    tm = o_ref.shape[0]
    for r0 in range(0, tm, OUT_SLAB):
        rows = slice(r0, r0 + OUT_SLAB)
        mb = jnp.dot(yb_ref[rows, :], wpb_ref[...], preferred_element_type=F32)
        m = ma_ref[rows, :].astype(F32) + gb_ref[rows, :].astype(F32) * mb
        r = jnp.dot(m.astype(BF16), wo_ref[...], preferred_element_type=F32)
        inv = lax.rsqrt(jnp.mean(r * r, axis=-1, keepdims=True) + EPS)
        o_ref[rows, :] = x_ref[rows, :] + r * inv * gpost_ref[...]


def _out_call(yb, gb, ma, x, wpb, wo, gpost):
    m = x.shape[0]
    tm = TM_OUT
    row = pl.BlockSpec((tm, D_MODEL), lambda i: (i, 0))
    return pl.pallas_call(
        _out_kernel,
        grid=(m // tm,),
        in_specs=[row, row, row, row,
                  _const_spec((D_B, D_MODEL)), _const_spec((D_MODEL, D_MODEL)),
                  _const_spec((1, D_MODEL))],
        out_specs=row,
        out_shape=jax.ShapeDtypeStruct((m, D_MODEL), F32),
        compiler_params=pltpu.CompilerParams(
            dimension_semantics=("arbitrary",), vmem_limit_bytes=48 * MIB),
        name="merge_out",
    )(yb, gb, ma, x, wpb, wo, gpost)


def kernel(x, w_in, b_gate, g_pre, g_post, sgu_ln_g, sgu_ln_b, w_spatial, b_spatial,
           w_proj_a, w_proj_b, w_out):
    batch, seq, d_model = x.shape
    depth = w_in.shape[0]
    assert d_model == D_MODEL and w_in.shape[2] == N_IN
    assert seq % (16 * 256) == 0 and (batch * seq) % TM_OUT == 0
    xf = x.reshape(batch * seq, d_model)
    for l in range(depth):
        w_in_l = w_in[l].astype(BF16)
        q, k, v, zb, gb = _qkv_call(xf, g_pre[l][None], w_in_l, b_gate[l][None])
        ma = _gmlp_call(xf, g_pre[l][None], w_in_l, b_gate[l][None],
                        sgu_ln_g[l][None], sgu_ln_b[l][None], w_spatial[l].astype(BF16),
                        b_spatial[l].T, w_proj_a[l].astype(BF16))
        yb = _attn_call(q, k, v, zb, batch, seq)
        xf = _out_call(yb, gb, ma, xf, w_proj_b[l].astype(BF16), w_out[l].astype(BF16),
                       g_post[l][None])
    return xf.reshape(batch, seq, d_model)
```

```python
import math

import jax
import jax.numpy as jnp
from jax import lax
from jax.experimental import pallas as pl
from jax.experimental.pallas import tpu as pltpu

D_MODEL = 1024
D_A = 2 * D_MODEL
CHUNK = 128
G_A = D_A // 128
N_HEADS = 8
HEAD_DIM = 128
D_B = N_HEADS * HEAD_DIM
DILATIONS = (1, 4, 16)
N_SIDE = 64
NEG = -1e30
EPS = 1e-6
LOG2E = math.log2(math.e)

OFF_U = 0
OFF_V = OFF_U + D_A
OFF_ZA = OFF_V + D_A
OFF_Q = OFF_ZA + D_A
OFF_K = OFF_Q + D_B
OFF_VB = OFF_K + D_B
OFF_ZB = OFF_VB + D_B
OFF_GA = OFF_ZB + D_B
OFF_GB = OFF_GA + D_MODEL
N_IN = OFF_GB + D_MODEL

QBLK = 128
KBLK = QBLK + 2 * N_SIDE
GROUP = 8
MIX_ROWS = 32
LAYOUT_ITERS = 4
TM_PROJ = 256
TM_OUT = 1024
OUT_SLAB = 256
CW = 512
MIB = 1024 * 1024

F32 = jnp.float32
BF16 = jnp.bfloat16


def _gelu(x):
    c = math.sqrt(2.0 / math.pi)
    return 0.5 * x * (1.0 + jnp.tanh(c * (x + 0.044715 * (x * x * x))))


def _sigmoid(x):
    return 0.5 * (jnp.tanh(0.5 * x) + 1.0)


def _silu(x):
    return x * _sigmoid(x)


def _const_spec(shape):
    nd = len(shape)
    return pl.BlockSpec(shape, lambda *_: (0,) * nd, pipeline_mode=pl.Buffered(1))


def _layer_spec(shape, layer):
    nd = len(shape)
    return pl.BlockSpec((None,) + tuple(shape), lambda *_: (layer,) + (0,) * nd,
                        pipeline_mode=pl.Buffered(1))


def _proj_kernel(x_ref, gpre_ref, w_in_ref, bgate_ref, lng_ref, lnb_ref, ws_ref, bst_ref, wpa_ref,
                 q_ref, k_ref, v_ref, zb_ref, gb_ref, ma_ref,
                 t_scr, v_scr, vn_scr, ya_scr):
    tm = x_ref.shape[0]
    x = x_ref[...]
    inv = lax.rsqrt(jnp.mean(x * x, axis=-1, keepdims=True) + EPS)
    h = (x * inv * gpre_ref[...]).astype(BF16)

    def proj(off, c):
        return jnp.dot(h, w_in_ref[:, off + c:off + c + CW], preferred_element_type=F32)

    for c in range(0, D_A, CW):
        v_scr[:, c:c + CW] = _gelu(proj(OFF_V, c))

    v = v_scr[...]
    mu = jnp.mean(v, axis=-1, keepdims=True)
    dv = v - mu
    var = jnp.mean(dv * dv, axis=-1, keepdims=True)
    vn_scr[...] = (dv * lax.rsqrt(var + EPS) * lng_ref[...] + lnb_ref[...]).astype(BF16)

    scale = LOG2E * HEAD_DIM ** -0.5

    def attn_branch(c):
        cs = slice(c, c + CW)
        gb_ref[:, cs] = _sigmoid(proj(OFF_GB, c) + bgate_ref[:, D_MODEL + c:D_MODEL + c + CW]).astype(BF16)
        zb_ref[:, cs] = _silu(proj(OFF_ZB, c)).astype(BF16)
        q_ref[:, cs] = proj(OFF_Q, c) * scale
        k_ref[:, cs] = proj(OFF_K, c)
        v_ref[:, cs] = proj(OFF_VB, c)

    for c in range(0, D_A, CW):
        t_scr[:, c:c + CW] = _gelu(proj(OFF_U, c)) * _silu(proj(OFF_ZA, c))
        if c % (2 * CW) == 0:
            attn_branch(c // 2)

    assert tm % (2 * CHUNK) == 0
    for c in range(0, tm, 2 * CHUNK):
        rows0 = slice(c, c + CHUNK)
        rows1 = slice(c + CHUNK, c + 2 * CHUNK)
        for g in range(G_A):
            cols = slice(g * 128, (g + 1) * 128)
            vn2 = jnp.concatenate([vn_scr[rows0, cols], vn_scr[rows1, cols]], axis=1)
            s2 = jnp.dot(ws_ref[g], vn2, preferred_element_type=F32) + bst_ref[:, g:g + 1]
            ya_scr[rows0, cols] = (t_scr[rows0, cols] * s2[:, :128]).astype(BF16)
            ya_scr[rows1, cols] = (t_scr[rows1, cols] * s2[:, 128:]).astype(BF16)

    for c in range(0, D_MODEL, CW):
        cs = slice(c, c + CW)
        ya_p = jnp.dot(ya_scr[...], wpa_ref[:, cs], preferred_element_type=F32)
        ma_ref[:, cs] = (_sigmoid(proj(OFF_GA, c) + bgate_ref[:, cs]) * ya_p).astype(BF16)


def _proj_call(layer, x, gpre, w_in, bgate, lng, lnb, ws, bst, wpa):
    m = x.shape[0]
    tm = TM_PROJ
    row = lambda i: (i, 0)
    out_f32 = jax.ShapeDtypeStruct((m, D_B), F32)
    out_bf16 = jax.ShapeDtypeStruct((m, D_B), BF16)
    return pl.pallas_call(
        _proj_kernel,
        grid=(m // tm,),
        in_specs=[
            pl.BlockSpec((tm, D_MODEL), row),
            _const_spec((1, D_MODEL)),
            _layer_spec((D_MODEL, N_IN), layer),
            _const_spec((1, 2 * D_MODEL)),
            _const_spec((1, D_A)),
            _const_spec((1, D_A)),
            _layer_spec((G_A, CHUNK, CHUNK), layer),
            _const_spec((CHUNK, G_A)),
            _layer_spec((D_A, D_MODEL), layer),
        ],
        out_specs=[pl.BlockSpec((tm, D_B), row)] * 6,
        out_shape=[out_f32, out_f32, out_f32, out_bf16, out_bf16, out_bf16],
        scratch_shapes=[
            pltpu.VMEM((tm, D_A), F32),
            pltpu.VMEM((tm, D_A), F32),
            pltpu.VMEM((tm, D_A), BF16),
            pltpu.VMEM((tm, D_A), BF16),
        ],
        compiler_params=pltpu.CompilerParams(
            dimension_semantics=("arbitrary",), vmem_limit_bytes=56 * MIB),
        name="proj_gmlp",
    )(x, gpre, w_in, bgate, lng, lnb, ws, bst, wpa)


def _attn_kernel(q_ref, k_ref, v_ref, zb_ref, y_ref, qd, kd, vd, f4q, f4k, f4v, o_scr, d_scr, m_scr,
                 s_scr, p_scr, bias_scr):
    seq = q_ref.shape[0]
    n_pat = len(DILATIONS)
    blocks_per_pat = seq // QBLK
    head = pl.program_id(1)

    slope = LOG2E * jnp.exp2(jnp.full((QBLK, KBLK), -1.0, F32) * (head + 1).astype(F32))
    qi = lax.broadcasted_iota(jnp.int32, (QBLK, KBLK), 0)
    ki = lax.broadcasted_iota(jnp.int32, (QBLK, KBLK), 1)
    for p, d in enumerate(DILATIONS):
        for case, shift in enumerate((N_SIDE, 0, KBLK - QBLK)):
            rel = jnp.abs(ki - shift - qi)
            bias_scr[3 * p + case] = jnp.where(rel <= N_SIDE, -slope * (d * rel).astype(F32), NEG)

    assert DILATIONS == (1, 4, 16)
    rows_per_copy = 256
    n_copies = seq // rows_per_copy
    srcs = (q_ref, k_ref, v_ref)
    dsts = (qd, kd, vd)
    mids = (f4q, f4k, f4v)

    def cast_natural(i):
        dst = slice(i * rows_per_copy, (i + 1) * rows_per_copy)
        for src, out in zip(srcs, dsts):
            out[0, dst, :] = src[dst, :].astype(BF16)

    def deinterleave4(i):
        dst = slice(i * rows_per_copy, (i + 1) * rows_per_copy)
        src4 = pl.ds(i // 4 + 4 * rows_per_copy * (i % 4), rows_per_copy, stride=4)
        for src, mid, out in zip(srcs, mids, dsts):
            x4 = src[src4, :]
            mid[dst, :] = x4
            out[1, dst, :] = x4.astype(BF16)

    def deinterleave16(i):
        dst = slice(i * rows_per_copy, (i + 1) * rows_per_copy)
        src = pl.ds((i % 4) * (seq // 4) + i // 4, rows_per_copy, stride=4)
        for mid, out in zip(mids, dsts):
            out[2, dst, :] = mid[src, :].astype(BF16)

    assert seq // 16 == rows_per_copy
    for i in range(n_copies):
        cast_natural(i)
    layout_work = [(deinterleave4, i) for i in range(n_copies)] + [(deinterleave16, i) for i in range(n_copies)]

    schedule = [(p, j) for p in (0, 2, 1) for j in range(blocks_per_pat)]
    assert sorted(schedule) == [(p, j) for p in range(n_pat) for j in range(blocks_per_pat)]
    od_slab = {0: 0, 2: 1}
    y_nat = f4q

    def block_addr(g):
        p, j = schedule[g]
        d = DILATIONS[p]
        sub_len = seq // d
        per_res = blocks_per_pat // d
        r, n = divmod(j, per_res)
        case = 1 if n == 0 else (2 if n == per_res - 1 else 0)
        q0 = j * QBLK
        k0 = r * sub_len + min(max(QBLK * n - N_SIDE, 0), sub_len - KBLK)
        if d == 16:
            out_rows = pl.ds((r % 4) * (seq // 4) + r // 4 + 4 * QBLK * n, QBLK, stride=4)
        else:
            out_rows = pl.ds(q0, QBLK)
        nat_rows = r + 4 * QBLK * n if d == 4 else None
        return p, q0, k0, 3 * p + case, out_rows, nat_rows

    def scores_stage(grp, slot):
        for b in range(GROUP):
            p, q0, k0, bias_idx, _, _ = block_addr(grp * GROUP + b)
            s_scr[slot, b * QBLK:(b + 1) * QBLK, :] = lax.dot_general(
                qd[p, q0:q0 + QBLK, :], kd[p, k0:k0 + KBLK, :],
                (((1,), (1,)), ((), ())), preferred_element_type=F32) + bias_scr[bias_idx]

    def softmax_stage(grp, slot):
        for b in range(GROUP):
            p, _, _, _, out_rows, _ = block_addr(grp * GROUP + b)
            rows = slice(b * QBLK, (b + 1) * QBLK)
            mx = jnp.max(s_scr[slot, rows, :], axis=-1, keepdims=True)
            p_scr[slot, rows, :] = jnp.exp2(s_scr[slot, rows, :] - mx).astype(BF16)
            m_scr[p, out_rows, :] = jnp.broadcast_to(mx, (QBLK, HEAD_DIM))

    ones = jnp.ones((KBLK, HEAD_DIM), BF16)

    def values_stage(grp, slot):
        for b in range(GROUP):
            p, q0, k0, _, out_rows, nat0 = block_addr(grp * GROUP + b)
            v_aug = jnp.concatenate([vd[p, k0:k0 + KBLK, :], ones], axis=1)
            oa = jnp.dot(p_scr[slot, b * QBLK:(b + 1) * QBLK, :], v_aug, preferred_element_type=F32)
            if nat0 is None:
                o_scr[od_slab[p], out_rows, :] = oa[:, :HEAD_DIM]
                d_scr[od_slab[p], out_rows, :] = oa[:, HEAD_DIM:]
                continue
            for c in range(0, QBLK, MIX_ROWS):
                by4 = pl.ds(q0 + c, MIX_ROWS)
                nat = pl.ds(nat0 + 4 * c, MIX_ROWS, stride=4)
                a1, a4, a16 = m_scr[0, nat, :], m_scr[1, by4, :], m_scr[2, by4, :]
                mx = jnp.maximum(jnp.maximum(a1, a4), a16)
                w1, w4, w16 = jnp.exp2(a1 - mx), jnp.exp2(a4 - mx), jnp.exp2(a16 - mx)
                o4, den4 = oa[c:c + MIX_ROWS, :HEAD_DIM], oa[c:c + MIX_ROWS, HEAD_DIM:]
                num = w1 * o_scr[0, nat, :] + w4 * o4 + w16 * o_scr[1, by4, :]
                den = w1 * d_scr[0, nat, :] + w4 * den4 + w16 * d_scr[1, by4, :]
                y_nat[nat, :] = num / den

    n_groups = n_pat * blocks_per_pat // GROUP
    per_iter = -(-len(layout_work) // LAYOUT_ITERS)
    for t in range(n_groups + 2):
        for fn, i in layout_work[t * per_iter:(t + 1) * per_iter]:
            fn(i)
        if 0 <= t - 2:
            values_stage(t - 2, t % 2)
        if 0 <= t - 1 < n_groups:
            softmax_stage(t - 1, (t - 1) % 2)
        if t < n_groups:
            scores_stage(t, t % 2)

    y_ref[...] = (y_nat[...] * zb_ref[...].astype(F32)).astype(BF16)


def _attn_call(q, k, v, zb, batch, seq):
    n_pat = len(DILATIONS)
    blk = pl.BlockSpec((seq, HEAD_DIM), lambda b, h: (b, h))
    pat_f32 = pltpu.VMEM((n_pat, seq, HEAD_DIM), F32)
    pat_bf16 = pltpu.VMEM((n_pat, seq, HEAD_DIM), BF16)
    return pl.pallas_call(
        _attn_kernel,
        grid=(batch, N_HEADS),
        in_specs=[blk, blk, blk, blk],
        out_specs=blk,
        out_shape=jax.ShapeDtypeStruct((batch * seq, D_B), BF16),
        scratch_shapes=[pat_bf16] * 3 + [pltpu.VMEM((seq, HEAD_DIM), F32)] * 3
        + [pltpu.VMEM((2, seq, HEAD_DIM), F32)] * 2 + [pat_f32]
        + [pltpu.VMEM((2, GROUP * QBLK, KBLK), F32),
           pltpu.VMEM((2, GROUP * QBLK, KBLK), BF16),
           pltpu.VMEM((3 * n_pat, QBLK, KBLK), F32)],
        compiler_params=pltpu.CompilerParams(
            dimension_semantics=("arbitrary", "arbitrary"), vmem_limit_bytes=58 * MIB),
        name="dilated_attn",
    )(q, k, v, zb)


def _out_kernel(yb_ref, gb_ref, ma_ref, x_ref, wpb_ref, wo_ref, gpost_ref, o_ref):
    tm = o_ref.shape[0]
    for r0 in range(0, tm, OUT_SLAB):
        rows = slice(r0, r0 + OUT_SLAB)
        mb = jnp.dot(yb_ref[rows, :], wpb_ref[...], preferred_element_type=F32)
        m = ma_ref[rows, :].astype(F32) + gb_ref[rows, :].astype(F32) * mb
        r = jnp.dot(m.astype(BF16), wo_ref[...], preferred_element_type=F32)
        inv = lax.rsqrt(jnp.mean(r * r, axis=-1, keepdims=True) + EPS)
        o_ref[rows, :] = x_ref[rows, :] + r * inv * gpost_ref[...]


def _out_call(layer, yb, gb, ma, x, wpb, wo, gpost):
    m = x.shape[0]
    tm = TM_OUT
    row = pl.BlockSpec((tm, D_MODEL), lambda i: (i, 0))
    return pl.pallas_call(
        _out_kernel,
        grid=(m // tm,),
        in_specs=[row, row, row, row,
                  _layer_spec((D_B, D_MODEL), layer), _layer_spec((D_MODEL, D_MODEL), layer),
                  _const_spec((1, D_MODEL))],
        out_specs=row,
        out_shape=jax.ShapeDtypeStruct((m, D_MODEL), F32),
        compiler_params=pltpu.CompilerParams(
            dimension_semantics=("arbitrary",), vmem_limit_bytes=48 * MIB),
        name="merge_out",
    )(yb, gb, ma, x, wpb, wo, gpost)


def kernel(x, w_in, b_gate, g_pre, g_post, sgu_ln_g, sgu_ln_b, w_spatial, b_spatial,
           w_proj_a, w_proj_b, w_out):
    batch, seq, d_model = x.shape
    depth = w_in.shape[0]
    assert d_model == D_MODEL and w_in.shape[2] == N_IN
    assert seq % (16 * 256) == 0 and (batch * seq) % TM_OUT == 0
    w_in, w_spatial, w_proj_a, w_proj_b, w_out = (
        w.astype(BF16) for w in (w_in, w_spatial, w_proj_a, w_proj_b, w_out))
    xf = x.reshape(batch * seq, d_model)
    for l in range(depth):
        q, k, v, zb, gb, ma = _proj_call(
            l, xf, g_pre[l][None], w_in, b_gate[l][None],
            sgu_ln_g[l][None], sgu_ln_b[l][None], w_spatial, b_spatial[l].T, w_proj_a)
        yb = _attn_call(q, k, v, zb, batch, seq)
        xf = _out_call(l, yb, gb, ma, xf, w_proj_b, w_out, g_post[l][None])
    return xf.reshape(batch, seq, d_model)
```

```python
import math

import jax
import jax.numpy as jnp
from jax import lax
from jax.experimental import pallas as pl
from jax.experimental.pallas import tpu as pltpu

D_MODEL = 1024
D_A = 2 * D_MODEL
CHUNK = 128
G_A = D_A // 128
N_HEADS = 8
HEAD_DIM = 128
D_B = N_HEADS * HEAD_DIM
DILATIONS = (1, 4, 16)
N_SIDE = 64
NEG = -1e30
EPS = 1e-6
LOG2E = math.log2(math.e)

OFF_U = 0
OFF_V = OFF_U + D_A
OFF_ZA = OFF_V + D_A
OFF_Q = OFF_ZA + D_A
OFF_K = OFF_Q + D_B
OFF_VB = OFF_K + D_B
OFF_ZB = OFF_VB + D_B
OFF_GA = OFF_ZB + D_B
OFF_GB = OFF_GA + D_MODEL
N_IN = OFF_GB + D_MODEL

QBLK = 128
KBLK = QBLK + 2 * N_SIDE
GROUP = 4
MIX_ROWS = 32
LAYOUT_ITERS = 8
TM_PROJ = 256
TM_OUT = 1024
OUT_SLAB = 256
CW = 512
MIB = 1024 * 1024

F32 = jnp.float32
BF16 = jnp.bfloat16


def _gelu(x):
    c = math.sqrt(2.0 / math.pi)
    return 0.5 * x * (1.0 + jnp.tanh(c * (x + 0.044715 * (x * x * x))))


def _sigmoid(x):
    return 0.5 * (jnp.tanh(0.5 * x) + 1.0)


def _silu(x):
    return x * _sigmoid(x)


def _const_spec(shape):
    nd = len(shape)
    return pl.BlockSpec(shape, lambda *_: (0,) * nd, pipeline_mode=pl.Buffered(1))


def _layer_spec(shape, layer):
    nd = len(shape)
    return pl.BlockSpec((None,) + tuple(shape), lambda *_: (layer,) + (0,) * nd,
                        pipeline_mode=pl.Buffered(1))


def _proj_kernel(x_ref, gpre_ref, w_in_ref, bgate_ref, lng_ref, lnb_ref, ws_ref, bst_ref, wpa_ref,
                 q_ref, k_ref, v_ref, zb_ref, gb_ref, ma_ref,
                 t_scr, v_scr, vn_scr, ya_scr):
    tm = x_ref.shape[0]
    x = x_ref[...]
    inv = lax.rsqrt(jnp.mean(x * x, axis=-1, keepdims=True) + EPS)
    h = (x * inv * gpre_ref[...]).astype(BF16)

    def proj(off, c):
        return jnp.dot(h, w_in_ref[:, off + c:off + c + CW], preferred_element_type=F32)

    for c in range(0, D_A, CW):
        v_scr[:, c:c + CW] = _gelu(proj(OFF_V, c))

    v = v_scr[...]
    mu = jnp.mean(v, axis=-1, keepdims=True)
    dv = v - mu
    var = jnp.mean(dv * dv, axis=-1, keepdims=True)
    vn_scr[...] = (dv * lax.rsqrt(var + EPS) * lng_ref[...] + lnb_ref[...]).astype(BF16)

    scale = LOG2E * HEAD_DIM ** -0.5

    def attn_branch(c):
        cs = slice(c, c + CW)
        gb_ref[:, cs] = _sigmoid(proj(OFF_GB, c) + bgate_ref[:, D_MODEL + c:D_MODEL + c + CW]).astype(BF16)
        zb_ref[:, cs] = _silu(proj(OFF_ZB, c)).astype(BF16)
        q_ref[:, cs] = proj(OFF_Q, c) * scale
        k_ref[:, cs] = proj(OFF_K, c)
        v_ref[:, cs] = proj(OFF_VB, c)

    for c in range(0, D_A, CW):
        t_scr[:, c:c + CW] = _gelu(proj(OFF_U, c)) * _silu(proj(OFF_ZA, c))
        if c % (2 * CW) == 0:
            attn_branch(c // 2)

    assert tm % (2 * CHUNK) == 0
    for c in range(0, tm, 2 * CHUNK):
        rows0 = slice(c, c + CHUNK)
        rows1 = slice(c + CHUNK, c + 2 * CHUNK)
        for g in range(G_A):
            cols = slice(g * 128, (g + 1) * 128)
            vn2 = jnp.concatenate([vn_scr[rows0, cols], vn_scr[rows1, cols]], axis=1)
            s2 = jnp.dot(ws_ref[g], vn2, preferred_element_type=F32) + bst_ref[:, g:g + 1]
            ya_scr[rows0, cols] = (t_scr[rows0, cols] * s2[:, :128]).astype(BF16)
            ya_scr[rows1, cols] = (t_scr[rows1, cols] * s2[:, 128:]).astype(BF16)

    for c in range(0, D_MODEL, CW):
        cs = slice(c, c + CW)
        ya_p = jnp.dot(ya_scr[...], wpa_ref[:, cs], preferred_element_type=F32)
        ma_ref[:, cs] = (_sigmoid(proj(OFF_GA, c) + bgate_ref[:, cs]) * ya_p).astype(BF16)


def _proj_call(layer, x, gpre, w_in, bgate, lng, lnb, ws, bst, wpa):
    m = x.shape[0]
    tm = TM_PROJ
    row = lambda i: (i, 0)
    out_f32 = jax.ShapeDtypeStruct((m, D_B), F32)
    out_bf16 = jax.ShapeDtypeStruct((m, D_B), BF16)
    return pl.pallas_call(
        _proj_kernel,
        grid=(m // tm,),
        in_specs=[
            pl.BlockSpec((tm, D_MODEL), row),
            _const_spec((1, D_MODEL)),
            _layer_spec((D_MODEL, N_IN), layer),
            _const_spec((1, 2 * D_MODEL)),
            _const_spec((1, D_A)),
            _const_spec((1, D_A)),
            _layer_spec((G_A, CHUNK, CHUNK), layer),
            _const_spec((CHUNK, G_A)),
            _layer_spec((D_A, D_MODEL), layer),
        ],
        out_specs=[pl.BlockSpec((tm, D_B), row)] * 6,
        out_shape=[out_f32, out_f32, out_f32, out_bf16, out_bf16, out_bf16],
        scratch_shapes=[
            pltpu.VMEM((tm, D_A), F32),
            pltpu.VMEM((tm, D_A), F32),
            pltpu.VMEM((tm, D_A), BF16),
            pltpu.VMEM((tm, D_A), BF16),
        ],
        compiler_params=pltpu.CompilerParams(
            dimension_semantics=("arbitrary",), vmem_limit_bytes=56 * MIB),
        name="proj_gmlp",
    )(x, gpre, w_in, bgate, lng, lnb, ws, bst, wpa)


def _attn_kernel(q_ref, k_ref, v_ref, zb_ref, y_ref, qd, kd, vd, f4q, f4k, f4v, o_scr, d_scr, m_scr,
                 s_scr, p_scr, bias_scr):
    seq = q_ref.shape[0]
    n_pat = len(DILATIONS)
    blocks_per_pat = seq // QBLK
    head = pl.program_id(1)

    slope = LOG2E * jnp.exp2(jnp.full((QBLK, KBLK), -1.0, F32) * (head + 1).astype(F32))
    qi = lax.broadcasted_iota(jnp.int32, (QBLK, KBLK), 0)
    ki = lax.broadcasted_iota(jnp.int32, (QBLK, KBLK), 1)
    for p, d in enumerate(DILATIONS):
        for case, shift in enumerate((N_SIDE, 0, KBLK - QBLK)):
            rel = jnp.abs(ki - shift - qi)
            bias_scr[3 * p + case] = jnp.where(rel <= N_SIDE, -slope * (d * rel).astype(F32), NEG)

    assert DILATIONS == (1, 4, 16)
    rows_per_copy = 256
    n_copies = seq // rows_per_copy
    srcs = (q_ref, k_ref, v_ref)
    dsts = (qd, kd, vd)
    mids = (f4q, f4k, f4v)

    def cast_natural(i):
        dst = slice(i * rows_per_copy, (i + 1) * rows_per_copy)
        for src, out in zip(srcs, dsts):
            out[0, dst, :] = src[dst, :].astype(BF16)

    def deinterleave4(i):
        dst = slice(i * rows_per_copy, (i + 1) * rows_per_copy)
        src4 = pl.ds(i // 4 + 4 * rows_per_copy * (i % 4), rows_per_copy, stride=4)
        for src, mid, out in zip(srcs, mids, dsts):
            x4 = src[src4, :]
            mid[dst, :] = x4
            out[1, dst, :] = x4.astype(BF16)

    def deinterleave16(i):
        dst = slice(i * rows_per_copy, (i + 1) * rows_per_copy)
        src = pl.ds((i % 4) * (seq // 4) + i // 4, rows_per_copy, stride=4)
        for mid, out in zip(mids, dsts):
            out[2, dst, :] = mid[src, :].astype(BF16)

    assert seq // 16 == rows_per_copy
    for i in range(n_copies):
        cast_natural(i)
    layout_work = [(deinterleave4, i) for i in range(n_copies)] + [(deinterleave16, i) for i in range(n_copies)]

    schedule = [(p, j) for p in (0, 2, 1) for j in range(blocks_per_pat)]
    assert sorted(schedule) == [(p, j) for p in range(n_pat) for j in range(blocks_per_pat)]
    od_slab = {0: 0, 2: 1}
    y_nat = f4q

    def block_addr(g):
        p, j = schedule[g]
        d = DILATIONS[p]
        sub_len = seq // d
        per_res = blocks_per_pat // d
        r, n = divmod(j, per_res)
        case = 1 if n == 0 else (2 if n == per_res - 1 else 0)
        q0 = j * QBLK
        k0 = r * sub_len + min(max(QBLK * n - N_SIDE, 0), sub_len - KBLK)
        if d == 16:
            out_rows = pl.ds((r % 4) * (seq // 4) + r // 4 + 4 * QBLK * n, QBLK, stride=4)
        else:
            out_rows = pl.ds(q0, QBLK)
        nat_rows = r + 4 * QBLK * n if d == 4 else None
        return p, q0, k0, 3 * p + case, out_rows, nat_rows

    def scores_stage(grp, slot):
        for b in range(GROUP):
            p, q0, k0, bias_idx, _, _ = block_addr(grp * GROUP + b)
            s_scr[slot, b * QBLK:(b + 1) * QBLK, :] = lax.dot_general(
                qd[p, q0:q0 + QBLK, :], kd[p, k0:k0 + KBLK, :],
                (((1,), (1,)), ((), ())), preferred_element_type=F32) + bias_scr[bias_idx]

    def softmax_stage(grp, slot):
        for b in range(GROUP):
            p, _, _, _, out_rows, _ = block_addr(grp * GROUP + b)
            rows = slice(b * QBLK, (b + 1) * QBLK)
            mx = jnp.max(s_scr[slot, rows, :], axis=-1, keepdims=True)
            p_scr[slot, rows, :] = jnp.exp2(s_scr[slot, rows, :] - mx).astype(BF16)
            m_scr[p, out_rows, :] = jnp.broadcast_to(mx, (QBLK, HEAD_DIM))

    ones = jnp.ones((KBLK, HEAD_DIM), BF16)

    def values_stage(grp, slot):
        for b in range(GROUP):
            p, q0, k0, _, out_rows, nat0 = block_addr(grp * GROUP + b)
            v_aug = jnp.concatenate([vd[p, k0:k0 + KBLK, :], ones], axis=1)
            oa = jnp.dot(p_scr[slot, b * QBLK:(b + 1) * QBLK, :], v_aug, preferred_element_type=F32)
            if nat0 is None:
                o_scr[od_slab[p], out_rows, :] = oa[:, :HEAD_DIM]
                d_scr[od_slab[p], out_rows, :] = oa[:, HEAD_DIM:]
                continue
            for c in range(0, QBLK, MIX_ROWS):
                by4 = pl.ds(q0 + c, MIX_ROWS)
                nat = pl.ds(nat0 + 4 * c, MIX_ROWS, stride=4)
                a1, a4, a16 = m_scr[0, nat, :], m_scr[1, by4, :], m_scr[2, by4, :]
                mx = jnp.maximum(jnp.maximum(a1, a4), a16)
                w1, w4, w16 = jnp.exp2(a1 - mx), jnp.exp2(a4 - mx), jnp.exp2(a16 - mx)
                o4, den4 = oa[c:c + MIX_ROWS, :HEAD_DIM], oa[c:c + MIX_ROWS, HEAD_DIM:]
                num = w1 * o_scr[0, nat, :] + w4 * o4 + w16 * o_scr[1, by4, :]
                den = w1 * d_scr[0, nat, :] + w4 * den4 + w16 * d_scr[1, by4, :]
                y_nat[nat, :] = num / den

    n_groups = n_pat * blocks_per_pat // GROUP
    per_iter = -(-len(layout_work) // LAYOUT_ITERS)
    for t in range(n_groups + 2):
        for fn, i in layout_work[t * per_iter:(t + 1) * per_iter]:
            fn(i)
        if 0 <= t - 2:
            values_stage(t - 2, t % 2)
        if 0 <= t - 1 < n_groups:
            softmax_stage(t - 1, (t - 1) % 2)
        if t < n_groups:
            scores_stage(t, t % 2)

    y_ref[...] = (y_nat[...] * zb_ref[...].astype(F32)).astype(BF16)


def _attn_call(q, k, v, zb, batch, seq):
    n_pat = len(DILATIONS)
    blk = pl.BlockSpec((seq, HEAD_DIM), lambda b, h: (b, h))
    pat_f32 = pltpu.VMEM((n_pat, seq, HEAD_DIM), F32)
    pat_bf16 = pltpu.VMEM((n_pat, seq, HEAD_DIM), BF16)
    return pl.pallas_call(
        _attn_kernel,
        grid=(batch, N_HEADS),
        in_specs=[blk, blk, blk, blk],
        out_specs=blk,
        out_shape=jax.ShapeDtypeStruct((batch * seq, D_B), BF16),
        scratch_shapes=[pat_bf16] * 3 + [pltpu.VMEM((seq, HEAD_DIM), F32)] * 3
        + [pltpu.VMEM((2, seq, HEAD_DIM), F32)] * 2 + [pat_f32]
        + [pltpu.VMEM((2, GROUP * QBLK, KBLK), F32),
           pltpu.VMEM((2, GROUP * QBLK, KBLK), BF16),
           pltpu.VMEM((3 * n_pat, QBLK, KBLK), F32)],
        compiler_params=pltpu.CompilerParams(
            dimension_semantics=("arbitrary", "arbitrary"), vmem_limit_bytes=58 * MIB),
        name="dilated_attn",
    )(q, k, v, zb)


def _out_kernel(yb_ref, gb_ref, ma_ref, x_ref, wpb_ref, wo_ref, gpost_ref, o_ref):
    tm = o_ref.shape[0]
    for r0 in range(0, tm, OUT_SLAB):
        rows = slice(r0, r0 + OUT_SLAB)
        mb = jnp.dot(yb_ref[rows, :], wpb_ref[...], preferred_element_type=F32)
        m = ma_ref[rows, :].astype(F32) + gb_ref[rows, :].astype(F32) * mb
        r = jnp.dot(m.astype(BF16), wo_ref[...], preferred_element_type=F32)
        inv = lax.rsqrt(jnp.mean(r * r, axis=-1, keepdims=True) + EPS)
        o_ref[rows, :] = x_ref[rows, :] + r * inv * gpost_ref[...]


def _out_call(layer, yb, gb, ma, x, wpb, wo, gpost):
    m = x.shape[0]
    tm = TM_OUT
    row = pl.BlockSpec((tm, D_MODEL), lambda i: (i, 0))
    return pl.pallas_call(
        _out_kernel,
        grid=(m // tm,),
        in_specs=[row, row, row, row,
                  _layer_spec((D_B, D_MODEL), layer), _layer_spec((D_MODEL, D_MODEL), layer),
                  _const_spec((1, D_MODEL))],
        out_specs=row,
        out_shape=jax.ShapeDtypeStruct((m, D_MODEL), F32),
        compiler_params=pltpu.CompilerParams(
            dimension_semantics=("arbitrary",), vmem_limit_bytes=48 * MIB),
        name="merge_out",
    )(yb, gb, ma, x, wpb, wo, gpost)


def kernel(x, w_in, b_gate, g_pre, g_post, sgu_ln_g, sgu_ln_b, w_spatial, b_spatial,
           w_proj_a, w_proj_b, w_out):
    batch, seq, d_model = x.shape
    depth = w_in.shape[0]
    assert d_model == D_MODEL and w_in.shape[2] == N_IN
    assert seq % (16 * 256) == 0 and (batch * seq) % TM_OUT == 0
    w_in, w_spatial, w_proj_a, w_proj_b, w_out = (
        w.astype(BF16) for w in (w_in, w_spatial, w_proj_a, w_proj_b, w_out))
    xf = x.reshape(batch * seq, d_model)
    for l in range(depth):
        q, k, v, zb, gb, ma = _proj_call(
            l, xf, g_pre[l][None], w_in, b_gate[l][None],
            sgu_ln_g[l][None], sgu_ln_b[l][None], w_spatial, b_spatial[l].T, w_proj_a)
        yb = _attn_call(q, k, v, zb, batch, seq)
        xf = _out_call(l, yb, gb, ma, xf, w_proj_b, w_out, g_post[l][None])
    return xf.reshape(batch, seq, d_model)
```

```python
import math

import jax
import jax.numpy as jnp
from jax import lax
from jax.experimental import pallas as pl
from jax.experimental.pallas import tpu as pltpu

D_MODEL = 1024
D_A = 2 * D_MODEL
CHUNK = 128
G_A = D_A // 128
N_HEADS = 8
HEAD_DIM = 128
D_B = N_HEADS * HEAD_DIM
DILATIONS = (1, 4, 16)
N_SIDE = 64
NEG = -1e30
EPS = 1e-6
LOG2E = math.log2(math.e)

OFF_U = 0
OFF_V = OFF_U + D_A
OFF_ZA = OFF_V + D_A
OFF_Q = OFF_ZA + D_A
OFF_K = OFF_Q + D_B
OFF_VB = OFF_K + D_B
OFF_ZB = OFF_VB + D_B
OFF_GA = OFF_ZB + D_B
OFF_GB = OFF_GA + D_MODEL
N_IN = OFF_GB + D_MODEL

QBLK = 128
KBLK = QBLK + 2 * N_SIDE
GROUP = 4
MIX_ROWS = 32
LAYOUT_ITERS = 8
TM_PROJ = 256
TM_OUT = 1024
OUT_SLAB = 256
CW = 512
MIB = 1024 * 1024

F32 = jnp.float32
BF16 = jnp.bfloat16


def _gelu(x):
    c = math.sqrt(2.0 / math.pi)
    return 0.5 * x * (1.0 + jnp.tanh(c * (x + 0.044715 * (x * x * x))))


def _sigmoid(x):
    return 0.5 * (jnp.tanh(0.5 * x) + 1.0)


def _silu(x):
    return x * _sigmoid(x)


def _const_spec(shape):
    nd = len(shape)
    return pl.BlockSpec(shape, lambda *_: (0,) * nd, pipeline_mode=pl.Buffered(1))


def _layer_spec(shape, layer):
    nd = len(shape)
    return pl.BlockSpec((None,) + tuple(shape), lambda *_: (layer,) + (0,) * nd,
                        pipeline_mode=pl.Buffered(1))


def _proj_kernel(x_ref, gpre_ref, w_in_ref, bgate_ref, lng_ref, lnb_ref, ws_ref, bst_ref, wpa_ref,
                 q_ref, k_ref, v_ref, zb_ref, gb_ref, ma_ref,
                 t_scr, v_scr, vn_scr, ya_scr):
    tm = x_ref.shape[0]
    x = x_ref[...]
    inv = lax.rsqrt(jnp.mean(x * x, axis=-1, keepdims=True) + EPS)
    h = (x * inv * gpre_ref[...]).astype(BF16)

    def proj(off, c):
        return jnp.dot(h, w_in_ref[:, off + c:off + c + CW], preferred_element_type=F32)

    for c in range(0, D_A, CW):
        v_scr[:, c:c + CW] = _gelu(proj(OFF_V, c))

    v = v_scr[...]
    mu = jnp.mean(v, axis=-1, keepdims=True)
    dv = v - mu
    var = jnp.mean(dv * dv, axis=-1, keepdims=True)
    vn_scr[...] = (dv * lax.rsqrt(var + EPS) * lng_ref[...] + lnb_ref[...]).astype(BF16)

    scale = LOG2E * HEAD_DIM ** -0.5

    def per_head(out_ref, val, c):
        for hd in range(c // HEAD_DIM, (c + CW) // HEAD_DIM):
            lo = hd * HEAD_DIM - c
            out_ref[hd] = val[:, lo:lo + HEAD_DIM]

    def attn_branch(c):
        cs = slice(c, c + CW)
        gb_ref[:, cs] = _sigmoid(proj(OFF_GB, c) + bgate_ref[:, D_MODEL + c:D_MODEL + c + CW]).astype(BF16)
        per_head(zb_ref, _silu(proj(OFF_ZB, c)).astype(BF16), c)
        per_head(q_ref, proj(OFF_Q, c) * scale, c)
        per_head(k_ref, proj(OFF_K, c), c)
        per_head(v_ref, proj(OFF_VB, c), c)

    for c in range(0, D_A, CW):
        t_scr[:, c:c + CW] = _gelu(proj(OFF_U, c)) * _silu(proj(OFF_ZA, c))
        if c % (2 * CW) == 0:
            attn_branch(c // 2)

    assert tm % (2 * CHUNK) == 0
    for c in range(0, tm, 2 * CHUNK):
        rows0 = slice(c, c + CHUNK)
        rows1 = slice(c + CHUNK, c + 2 * CHUNK)
        for g in range(G_A):
            cols = slice(g * 128, (g + 1) * 128)
            vn2 = jnp.concatenate([vn_scr[rows0, cols], vn_scr[rows1, cols]], axis=1)
            s2 = jnp.dot(ws_ref[g], vn2, preferred_element_type=F32) + bst_ref[:, g:g + 1]
            ya_scr[rows0, cols] = (t_scr[rows0, cols] * s2[:, :128]).astype(BF16)
            ya_scr[rows1, cols] = (t_scr[rows1, cols] * s2[:, 128:]).astype(BF16)

    for c in range(0, D_MODEL, CW):
        cs = slice(c, c + CW)
        ya_p = jnp.dot(ya_scr[...], wpa_ref[:, cs], preferred_element_type=F32)
        ma_ref[:, cs] = (_sigmoid(proj(OFF_GA, c) + bgate_ref[:, cs]) * ya_p).astype(BF16)


def _proj_call(layer, x, gpre, w_in, bgate, lng, lnb, ws, bst, wpa):
    m = x.shape[0]
    tm = TM_PROJ
    row = lambda i: (i, 0)
    heads = pl.BlockSpec((N_HEADS, tm, HEAD_DIM), lambda i: (0, i, 0))
    heads_f32 = jax.ShapeDtypeStruct((N_HEADS, m, HEAD_DIM), F32)
    heads_bf16 = jax.ShapeDtypeStruct((N_HEADS, m, HEAD_DIM), BF16)
    out_bf16 = jax.ShapeDtypeStruct((m, D_B), BF16)
    return pl.pallas_call(
        _proj_kernel,
        grid=(m // tm,),
        in_specs=[
            pl.BlockSpec((tm, D_MODEL), row),
            _const_spec((1, D_MODEL)),
            _layer_spec((D_MODEL, N_IN), layer),
            _const_spec((1, 2 * D_MODEL)),
            _const_spec((1, D_A)),
            _const_spec((1, D_A)),
            _layer_spec((G_A, CHUNK, CHUNK), layer),
            _const_spec((CHUNK, G_A)),
            _layer_spec((D_A, D_MODEL), layer),
        ],
        out_specs=[heads] * 4 + [pl.BlockSpec((tm, D_B), row)] * 2,
        out_shape=[heads_f32, heads_f32, heads_f32, heads_bf16, out_bf16, out_bf16],
        scratch_shapes=[
            pltpu.VMEM((tm, D_A), F32),
            pltpu.VMEM((tm, D_A), F32),
            pltpu.VMEM((tm, D_A), BF16),
            pltpu.VMEM((tm, D_A), BF16),
        ],
        compiler_params=pltpu.CompilerParams(
            dimension_semantics=("arbitrary",), vmem_limit_bytes=56 * MIB),
        name="proj_gmlp",
    )(x, gpre, w_in, bgate, lng, lnb, ws, bst, wpa)


def _attn_kernel(q_ref, k_ref, v_ref, zb_ref, y_ref, qd, kd, vd, f4q, f4k, f4v, o_scr, d_scr, m_scr,
                 s_scr, p_scr, bias_scr):
    seq = q_ref.shape[0]
    n_pat = len(DILATIONS)
    blocks_per_pat = seq // QBLK
    head = pl.program_id(1)

    slope = LOG2E * jnp.exp2(jnp.full((QBLK, KBLK), -1.0, F32) * (head + 1).astype(F32))
    qi = lax.broadcasted_iota(jnp.int32, (QBLK, KBLK), 0)
    ki = lax.broadcasted_iota(jnp.int32, (QBLK, KBLK), 1)
    for p, d in enumerate(DILATIONS):
        for case, shift in enumerate((N_SIDE, 0, KBLK - QBLK)):
            rel = jnp.abs(ki - shift - qi)
            bias_scr[3 * p + case] = jnp.where(rel <= N_SIDE, -slope * (d * rel).astype(F32), NEG)

    assert DILATIONS == (1, 4, 16)
    rows_per_copy = 256
    n_copies = seq // rows_per_copy
    srcs = (q_ref, k_ref, v_ref)
    dsts = (qd, kd, vd)
    mids = (f4q, f4k, f4v)

    def cast_natural(i):
        dst = slice(i * rows_per_copy, (i + 1) * rows_per_copy)
        for src, out in zip(srcs, dsts):
            out[0, dst, :] = src[dst, :].astype(BF16)

    def deinterleave4(i):
        dst = slice(i * rows_per_copy, (i + 1) * rows_per_copy)
        src4 = pl.ds(i // 4 + 4 * rows_per_copy * (i % 4), rows_per_copy, stride=4)
        for src, mid, out in zip(srcs, mids, dsts):
            x4 = src[src4, :]
            mid[dst, :] = x4
            out[1, dst, :] = x4.astype(BF16)

    def deinterleave16(i):
        dst = slice(i * rows_per_copy, (i + 1) * rows_per_copy)
        src = pl.ds((i % 4) * (seq // 4) + i // 4, rows_per_copy, stride=4)
        for mid, out in zip(mids, dsts):
            out[2, dst, :] = mid[src, :].astype(BF16)

    assert seq // 16 == rows_per_copy
    for i in range(n_copies):
        cast_natural(i)
    layout_work = [(deinterleave4, i) for i in range(n_copies)] + [(deinterleave16, i) for i in range(n_copies)]

    schedule = [(p, j) for p in (0, 2, 1) for j in range(blocks_per_pat)]
    assert sorted(schedule) == [(p, j) for p in range(n_pat) for j in range(blocks_per_pat)]
    od_slab = {0: 0, 2: 1}
    y_nat = f4q

    def block_addr(g):
        p, j = schedule[g]
        d = DILATIONS[p]
        sub_len = seq // d
        per_res = blocks_per_pat // d
        r, n = divmod(j, per_res)
        case = 1 if n == 0 else (2 if n == per_res - 1 else 0)
        q0 = j * QBLK
        k0 = r * sub_len + min(max(QBLK * n - N_SIDE, 0), sub_len - KBLK)
        if d == 16:
            out_rows = pl.ds((r % 4) * (seq // 4) + r // 4 + 4 * QBLK * n, QBLK, stride=4)
        else:
            out_rows = pl.ds(q0, QBLK)
        nat_rows = r + 4 * QBLK * n if d == 4 else None
        return p, q0, k0, 3 * p + case, out_rows, nat_rows

    def scores_stage(grp, slot):
        for b in range(GROUP):
            p, q0, k0, bias_idx, _, _ = block_addr(grp * GROUP + b)
            s_scr[slot, b * QBLK:(b + 1) * QBLK, :] = lax.dot_general(
                qd[p, q0:q0 + QBLK, :], kd[p, k0:k0 + KBLK, :],
                (((1,), (1,)), ((), ())), preferred_element_type=F32) + bias_scr[bias_idx]

    def softmax_stage(grp, slot):
        for b in range(GROUP):
            p, _, _, _, out_rows, _ = block_addr(grp * GROUP + b)
            rows = slice(b * QBLK, (b + 1) * QBLK)
            mx = jnp.max(s_scr[slot, rows, :], axis=-1, keepdims=True)
            p_scr[slot, rows, :] = jnp.exp2(s_scr[slot, rows, :] - mx).astype(BF16)
            m_scr[p, out_rows, :] = jnp.broadcast_to(mx, (QBLK, HEAD_DIM))

    ones = jnp.ones((KBLK, HEAD_DIM), BF16)

    def values_stage(grp, slot):
        for b in range(GROUP):
            p, q0, k0, _, out_rows, nat0 = block_addr(grp * GROUP + b)
            v_aug = jnp.concatenate([vd[p, k0:k0 + KBLK, :], ones], axis=1)
            oa = jnp.dot(p_scr[slot, b * QBLK:(b + 1) * QBLK, :], v_aug, preferred_element_type=F32)
            if nat0 is None:
                o_scr[od_slab[p], out_rows, :] = oa[:, :HEAD_DIM]
                d_scr[od_slab[p], out_rows, :] = oa[:, HEAD_DIM:]
                continue
            for c in range(0, QBLK, MIX_ROWS):
                by4 = pl.ds(q0 + c, MIX_ROWS)
                nat = pl.ds(nat0 + 4 * c, MIX_ROWS, stride=4)
                a1, a4, a16 = m_scr[0, nat, :], m_scr[1, by4, :], m_scr[2, by4, :]
                mx = jnp.maximum(jnp.maximum(a1, a4), a16)
                w1, w4, w16 = jnp.exp2(a1 - mx), jnp.exp2(a4 - mx), jnp.exp2(a16 - mx)
                o4, den4 = oa[c:c + MIX_ROWS, :HEAD_DIM], oa[c:c + MIX_ROWS, HEAD_DIM:]
                num = w1 * o_scr[0, nat, :] + w4 * o4 + w16 * o_scr[1, by4, :]
                den = w1 * d_scr[0, nat, :] + w4 * den4 + w16 * d_scr[1, by4, :]
                y_nat[nat, :] = num / den

    n_groups = n_pat * blocks_per_pat // GROUP
    per_iter = -(-len(layout_work) // LAYOUT_ITERS)
    for t in range(n_groups + 2):
        for fn, i in layout_work[t * per_iter:(t + 1) * per_iter]:
            fn(i)
        if 0 <= t - 2:
            values_stage(t - 2, t % 2)
        if 0 <= t - 1 < n_groups:
            softmax_stage(t - 1, (t - 1) % 2)
        if t < n_groups:
            scores_stage(t, t % 2)

    y_ref[...] = (y_nat[...] * zb_ref[...].astype(F32)).astype(BF16)


def _attn_call(q, k, v, zb, batch, seq):
    n_pat = len(DILATIONS)
    blk = pl.BlockSpec((seq, HEAD_DIM), lambda b, h: (b, h))
    slab = pl.BlockSpec((None, seq, HEAD_DIM), lambda b, h: (h, b, 0))
    pat_f32 = pltpu.VMEM((n_pat, seq, HEAD_DIM), F32)
    pat_bf16 = pltpu.VMEM((n_pat, seq, HEAD_DIM), BF16)
    return pl.pallas_call(
        _attn_kernel,
        grid=(batch, N_HEADS),
        in_specs=[slab, slab, slab, slab],
        out_specs=blk,
        out_shape=jax.ShapeDtypeStruct((batch * seq, D_B), BF16),
        scratch_shapes=[pat_bf16] * 3 + [pltpu.VMEM((seq, HEAD_DIM), F32)] * 3
        + [pltpu.VMEM((2, seq, HEAD_DIM), F32)] * 2 + [pat_f32]
        + [pltpu.VMEM((2, GROUP * QBLK, KBLK), F32),
           pltpu.VMEM((2, GROUP * QBLK, KBLK), BF16),
           pltpu.VMEM((3 * n_pat, QBLK, KBLK), F32)],
        compiler_params=pltpu.CompilerParams(
            dimension_semantics=("arbitrary", "arbitrary"), vmem_limit_bytes=58 * MIB),
        name="dilated_attn",
    )(q, k, v, zb)


def _out_kernel(yb_ref, gb_ref, ma_ref, x_ref, wpb_ref, wo_ref, gpost_ref, o_ref):
    tm = o_ref.shape[0]
    for r0 in range(0, tm, OUT_SLAB):
        rows = slice(r0, r0 + OUT_SLAB)
        mb = jnp.dot(yb_ref[rows, :], wpb_ref[...], preferred_element_type=F32)
        m = ma_ref[rows, :].astype(F32) + gb_ref[rows, :].astype(F32) * mb
        r = jnp.dot(m.astype(BF16), wo_ref[...], preferred_element_type=F32)
        inv = lax.rsqrt(jnp.mean(r * r, axis=-1, keepdims=True) + EPS)
        o_ref[rows, :] = x_ref[rows, :] + r * inv * gpost_ref[...]


def _out_call(layer, yb, gb, ma, x, wpb, wo, gpost):
    m = x.shape[0]
    tm = TM_OUT
    row = pl.BlockSpec((tm, D_MODEL), lambda i: (i, 0))
    return pl.pallas_call(
        _out_kernel,
        grid=(m // tm,),
        in_specs=[row, row, row, row,
                  _layer_spec((D_B, D_MODEL), layer), _layer_spec((D_MODEL, D_MODEL), layer),
                  _const_spec((1, D_MODEL))],
        out_specs=row,
        out_shape=jax.ShapeDtypeStruct((m, D_MODEL), F32),
        compiler_params=pltpu.CompilerParams(
            dimension_semantics=("arbitrary",), vmem_limit_bytes=48 * MIB),
        name="merge_out",
    )(yb, gb, ma, x, wpb, wo, gpost)


def kernel(x, w_in, b_gate, g_pre, g_post, sgu_ln_g, sgu_ln_b, w_spatial, b_spatial,
           w_proj_a, w_proj_b, w_out):
    batch, seq, d_model = x.shape
    depth = w_in.shape[0]
    assert d_model == D_MODEL and w_in.shape[2] == N_IN
    assert seq % (16 * 256) == 0 and (batch * seq) % TM_OUT == 0
    w_in, w_spatial, w_proj_a, w_proj_b, w_out = (
        w.astype(BF16) for w in (w_in, w_spatial, w_proj_a, w_proj_b, w_out))
    xf = x.reshape(batch * seq, d_model)
    for l in range(depth):
        q, k, v, zb, gb, ma = _proj_call(
            l, xf, g_pre[l][None], w_in, b_gate[l][None],
            sgu_ln_g[l][None], sgu_ln_b[l][None], w_spatial, b_spatial[l].T, w_proj_a)
        yb = _attn_call(q, k, v, zb, batch, seq)
        xf = _out_call(l, yb, gb, ma, xf, w_proj_b, w_out, g_post[l][None])
    return xf.reshape(batch, seq, d_model)
```

```python
import math

import jax
import jax.numpy as jnp
from jax import lax
from jax.experimental import pallas as pl
from jax.experimental.pallas import tpu as pltpu

D_MODEL = 1024
D_A = 2 * D_MODEL
CHUNK = 128
G_A = D_A // 128
N_HEADS = 8
HEAD_DIM = 128
D_B = N_HEADS * HEAD_DIM
DILATIONS = (1, 4, 16)
N_SIDE = 64
NEG = -1e30
EPS = 1e-6
LOG2E = math.log2(math.e)

OFF_U = 0
OFF_V = OFF_U + D_A
OFF_ZA = OFF_V + D_A
OFF_Q = OFF_ZA + D_A
OFF_K = OFF_Q + D_B
OFF_VB = OFF_K + D_B
OFF_ZB = OFF_VB + D_B
OFF_GA = OFF_ZB + D_B
OFF_GB = OFF_GA + D_MODEL
N_IN = OFF_GB + D_MODEL

QBLK = 128
KBLK = QBLK + 2 * N_SIDE
GROUP = 4
MIX_ROWS = 32
LAYOUT_ITERS = 8
TM_PROJ = 256
TM_OUT = 1024
OUT_SLAB = 256
CW = 512
MIB = 1024 * 1024

F32 = jnp.float32
BF16 = jnp.bfloat16


def _gelu(x):
    c = math.sqrt(2.0 / math.pi)
    return 0.5 * x * (1.0 + jnp.tanh(c * (x + 0.044715 * (x * x * x))))


def _sigmoid(x):
    return 0.5 * (jnp.tanh(0.5 * x) + 1.0)


def _silu(x):
    return x * _sigmoid(x)


def _const_spec(shape):
    nd = len(shape)
    return pl.BlockSpec(shape, lambda *_: (0,) * nd, pipeline_mode=pl.Buffered(1))


def _layer_spec(shape, layer):
    nd = len(shape)
    return pl.BlockSpec((None,) + tuple(shape), lambda *_: (layer,) + (0,) * nd,
                        pipeline_mode=pl.Buffered(1))


def _proj_kernel(x_ref, gpre_ref, w_in_ref, bgate_ref, lng_ref, lnb_ref, ws_ref, bst_ref, wpa_ref,
                 q_ref, k_ref, v_ref, zb_ref, gb_ref, ma_ref,
                 t_scr, v_scr, vn_scr, ya_scr):
    tm = x_ref.shape[0]
    x = x_ref[...]
    inv = lax.rsqrt(jnp.mean(x * x, axis=-1, keepdims=True) + EPS)
    h = (x * inv * gpre_ref[...]).astype(BF16)

    def proj(off, c):
        return jnp.dot(h, w_in_ref[:, off + c:off + c + CW], preferred_element_type=F32)

    for c in range(0, D_A, CW):
        v_scr[:, c:c + CW] = _gelu(proj(OFF_V, c))

    v = v_scr[...]
    mu = jnp.mean(v, axis=-1, keepdims=True)
    dv = v - mu
    var = jnp.mean(dv * dv, axis=-1, keepdims=True)
    vn_scr[...] = (dv * lax.rsqrt(var + EPS) * lng_ref[...] + lnb_ref[...]).astype(BF16)

    scale = LOG2E * HEAD_DIM ** -0.5

    def per_head(out_ref, val, c):
        for hd in range(c // HEAD_DIM, (c + val.shape[1]) // HEAD_DIM):
            lo = hd * HEAD_DIM - c
            out_ref[hd] = val[:, lo:lo + HEAD_DIM]

    def whole(off):
        return jnp.dot(h, w_in_ref[:, off:off + D_B], preferred_element_type=F32)

    def attn_branch(c):
        cs = slice(c, c + CW)
        gb_ref[:, cs] = _sigmoid(proj(OFF_GB, c) + bgate_ref[:, D_MODEL + c:D_MODEL + c + CW]).astype(BF16)
        per_head(zb_ref, _silu(proj(OFF_ZB, c)).astype(BF16), c)
        if c == 0:
            per_head(q_ref, whole(OFF_Q) * scale, 0)
        else:
            per_head(k_ref, whole(OFF_K), 0)
            per_head(v_ref, whole(OFF_VB), 0)

    assert D_B == 2 * CW
    for c in range(0, D_A, CW):
        t_scr[:, c:c + CW] = _gelu(proj(OFF_U, c)) * _silu(proj(OFF_ZA, c))
        if c % (2 * CW) == 0:
            attn_branch(c // 2)

    assert tm % (2 * CHUNK) == 0
    for c in range(0, tm, 2 * CHUNK):
        rows0 = slice(c, c + CHUNK)
        rows1 = slice(c + CHUNK, c + 2 * CHUNK)
        for g in range(G_A):
            cols = slice(g * 128, (g + 1) * 128)
            vn2 = jnp.concatenate([vn_scr[rows0, cols], vn_scr[rows1, cols]], axis=1)
            s2 = jnp.dot(ws_ref[g], vn2, preferred_element_type=F32) + bst_ref[:, g:g + 1]
            ya_scr[rows0, cols] = (t_scr[rows0, cols] * s2[:, :128]).astype(BF16)
            ya_scr[rows1, cols] = (t_scr[rows1, cols] * s2[:, 128:]).astype(BF16)

    for c in range(0, D_MODEL, CW):
        cs = slice(c, c + CW)
        ya_p = jnp.dot(ya_scr[...], wpa_ref[:, cs], preferred_element_type=F32)
        ma_ref[:, cs] = (_sigmoid(proj(OFF_GA, c) + bgate_ref[:, cs]) * ya_p).astype(BF16)


def _proj_call(layer, x, gpre, w_in, bgate, lng, lnb, ws, bst, wpa):
    m = x.shape[0]
    tm = TM_PROJ
    row = lambda i: (i, 0)
    heads = pl.BlockSpec((N_HEADS, tm, HEAD_DIM), lambda i: (0, i, 0))
    heads_f32 = jax.ShapeDtypeStruct((N_HEADS, m, HEAD_DIM), F32)
    heads_bf16 = jax.ShapeDtypeStruct((N_HEADS, m, HEAD_DIM), BF16)
    out_bf16 = jax.ShapeDtypeStruct((m, D_B), BF16)
    return pl.pallas_call(
        _proj_kernel,
        grid=(m // tm,),
        in_specs=[
            pl.BlockSpec((tm, D_MODEL), row),
            _const_spec((1, D_MODEL)),
            _layer_spec((D_MODEL, N_IN), layer),
            _const_spec((1, 2 * D_MODEL)),
            _const_spec((1, D_A)),
            _const_spec((1, D_A)),
            _layer_spec((G_A, CHUNK, CHUNK), layer),
            _const_spec((CHUNK, G_A)),
            _layer_spec((D_A, D_MODEL), layer),
        ],
        out_specs=[heads] * 4 + [pl.BlockSpec((tm, D_B), row)] * 2,
        out_shape=[heads_f32, heads_f32, heads_f32, heads_bf16, out_bf16, out_bf16],
        scratch_shapes=[
            pltpu.VMEM((tm, D_A), F32),
            pltpu.VMEM((tm, D_A), F32),
            pltpu.VMEM((tm, D_A), BF16),
            pltpu.VMEM((tm, D_A), BF16),
        ],
        compiler_params=pltpu.CompilerParams(
            dimension_semantics=("arbitrary",), vmem_limit_bytes=56 * MIB),
        name="proj_gmlp",
    )(x, gpre, w_in, bgate, lng, lnb, ws, bst, wpa)


def _attn_kernel(q_ref, k_ref, v_ref, zb_ref, y_ref, qd, kd, vd, f4q, f4k, f4v, o_scr, d_scr, m_scr,
                 s_scr, p_scr, bias_scr):
    seq = q_ref.shape[0]
    n_pat = len(DILATIONS)
    blocks_per_pat = seq // QBLK
    head = pl.program_id(1)

    slope = LOG2E * jnp.exp2(jnp.full((QBLK, KBLK), -1.0, F32) * (head + 1).astype(F32))
    qi = lax.broadcasted_iota(jnp.int32, (QBLK, KBLK), 0)
    ki = lax.broadcasted_iota(jnp.int32, (QBLK, KBLK), 1)
    for p, d in enumerate(DILATIONS):
        for case, shift in enumerate((N_SIDE, 0, KBLK - QBLK)):
            rel = jnp.abs(ki - shift - qi)
            bias_scr[3 * p + case] = jnp.where(rel <= N_SIDE, -slope * (d * rel).astype(F32), NEG)

    assert DILATIONS == (1, 4, 16)
    rows_per_copy = 256
    n_copies = seq // rows_per_copy
    srcs = (q_ref, k_ref, v_ref)
    dsts = (qd, kd, vd)
    mids = (f4q, f4k, f4v)

    def cast_natural(i):
        dst = slice(i * rows_per_copy, (i + 1) * rows_per_copy)
        for src, out in zip(srcs, dsts):
            out[0, dst, :] = src[dst, :].astype(BF16)

    def deinterleave4(i):
        dst = slice(i * rows_per_copy, (i + 1) * rows_per_copy)
        src4 = pl.ds(i // 4 + 4 * rows_per_copy * (i % 4), rows_per_copy, stride=4)
        for src, mid, out in zip(srcs, mids, dsts):
            x4 = src[src4, :]
            mid[dst, :] = x4
            out[1, dst, :] = x4.astype(BF16)

    def deinterleave16(i):
        dst = slice(i * rows_per_copy, (i + 1) * rows_per_copy)
        src = pl.ds((i % 4) * (seq // 4) + i // 4, rows_per_copy, stride=4)
        for mid, out in zip(mids, dsts):
            out[2, dst, :] = mid[src, :].astype(BF16)

    assert seq // 16 == rows_per_copy
    for i in range(n_copies):
        cast_natural(i)
    layout_work = [(deinterleave4, i) for i in range(n_copies)] + [(deinterleave16, i) for i in range(n_copies)]

    schedule = [(p, j) for p in (0, 2, 1) for j in range(blocks_per_pat)]
    assert sorted(schedule) == [(p, j) for p in range(n_pat) for j in range(blocks_per_pat)]
    od_slab = {0: 0, 2: 1}
    y_nat = f4q

    def block_addr(g):
        p, j = schedule[g]
        d = DILATIONS[p]
        sub_len = seq // d
        per_res = blocks_per_pat // d
        r, n = divmod(j, per_res)
        case = 1 if n == 0 else (2 if n == per_res - 1 else 0)
        q0 = j * QBLK
        k0 = r * sub_len + min(max(QBLK * n - N_SIDE, 0), sub_len - KBLK)
        if d == 16:
            out_rows = pl.ds((r % 4) * (seq // 4) + r // 4 + 4 * QBLK * n, QBLK, stride=4)
        else:
            out_rows = pl.ds(q0, QBLK)
        nat_rows = r + 4 * QBLK * n if d == 4 else None
        return p, q0, k0, 3 * p + case, out_rows, nat_rows

    def scores_stage(grp, slot):
        for b in range(GROUP):
            p, q0, k0, bias_idx, _, _ = block_addr(grp * GROUP + b)
            s_scr[slot, b * QBLK:(b + 1) * QBLK, :] = lax.dot_general(
                qd[p, q0:q0 + QBLK, :], kd[p, k0:k0 + KBLK, :],
                (((1,), (1,)), ((), ())), preferred_element_type=F32) + bias_scr[bias_idx]

    def softmax_stage(grp, slot):
        for b in range(GROUP):
            p, _, _, _, out_rows, _ = block_addr(grp * GROUP + b)
            rows = slice(b * QBLK, (b + 1) * QBLK)
            mx = jnp.max(s_scr[slot, rows, :], axis=-1, keepdims=True)
            p_scr[slot, rows, :] = jnp.exp2(s_scr[slot, rows, :] - mx).astype(BF16)
            m_scr[p, out_rows, :] = jnp.broadcast_to(mx, (QBLK, HEAD_DIM))

    ones = jnp.ones((KBLK, HEAD_DIM), BF16)

    def values_stage(grp, slot):
        for b in range(GROUP):
            p, q0, k0, _, out_rows, nat0 = block_addr(grp * GROUP + b)
            v_aug = jnp.concatenate([vd[p, k0:k0 + KBLK, :], ones], axis=1)
            oa = jnp.dot(p_scr[slot, b * QBLK:(b + 1) * QBLK, :], v_aug, preferred_element_type=F32)
            if nat0 is None:
                o_scr[od_slab[p], out_rows, :] = oa[:, :HEAD_DIM]
                d_scr[od_slab[p], out_rows, :] = oa[:, HEAD_DIM:]
                continue
            for c in range(0, QBLK, MIX_ROWS):
                by4 = pl.ds(q0 + c, MIX_ROWS)
                nat = pl.ds(nat0 + 4 * c, MIX_ROWS, stride=4)
                a1, a4, a16 = m_scr[0, nat, :], m_scr[1, by4, :], m_scr[2, by4, :]
                mx = jnp.maximum(jnp.maximum(a1, a4), a16)
                w1, w4, w16 = jnp.exp2(a1 - mx), jnp.exp2(a4 - mx), jnp.exp2(a16 - mx)
                o4, den4 = oa[c:c + MIX_ROWS, :HEAD_DIM], oa[c:c + MIX_ROWS, HEAD_DIM:]
                num = w1 * o_scr[0, nat, :] + w4 * o4 + w16 * o_scr[1, by4, :]
                den = w1 * d_scr[0, nat, :] + w4 * den4 + w16 * d_scr[1, by4, :]
                y_nat[nat, :] = num / den

    n_groups = n_pat * blocks_per_pat // GROUP
    per_iter = -(-len(layout_work) // LAYOUT_ITERS)
    for t in range(n_groups + 2):
        for fn, i in layout_work[t * per_iter:(t + 1) * per_iter]:
            fn(i)
        if 0 <= t - 2:
            values_stage(t - 2, t % 2)
        if 0 <= t - 1 < n_groups:
            softmax_stage(t - 1, (t - 1) % 2)
        if t < n_groups:
            scores_stage(t, t % 2)

    y_ref[...] = (y_nat[...] * zb_ref[...].astype(F32)).astype(BF16)


def _attn_call(q, k, v, zb, batch, seq):
    n_pat = len(DILATIONS)
    blk = pl.BlockSpec((seq, HEAD_DIM), lambda b, h: (b, h))
    slab = pl.BlockSpec((None, seq, HEAD_DIM), lambda b, h: (h, b, 0))
    pat_f32 = pltpu.VMEM((n_pat, seq, HEAD_DIM), F32)
    pat_bf16 = pltpu.VMEM((n_pat, seq, HEAD_DIM), BF16)
    return pl.pallas_call(
        _attn_kernel,
        grid=(batch, N_HEADS),
        in_specs=[slab, slab, slab, slab],
        out_specs=blk,
        out_shape=jax.ShapeDtypeStruct((batch * seq, D_B), BF16),
        scratch_shapes=[pat_bf16] * 3 + [pltpu.VMEM((seq, HEAD_DIM), F32)] * 3
        + [pltpu.VMEM((2, seq, HEAD_DIM), F32)] * 2 + [pat_f32]
        + [pltpu.VMEM((2, GROUP * QBLK, KBLK), F32),
           pltpu.VMEM((2, GROUP * QBLK, KBLK), BF16),
           pltpu.VMEM((3 * n_pat, QBLK, KBLK), F32)],
        compiler_params=pltpu.CompilerParams(
            dimension_semantics=("arbitrary", "arbitrary"), vmem_limit_bytes=58 * MIB),
        name="dilated_attn",
    )(q, k, v, zb)


def _out_kernel(yb_ref, gb_ref, ma_ref, x_ref, wpb_ref, wo_ref, gpost_ref, o_ref):
    tm = o_ref.shape[0]
    for r0 in range(0, tm, OUT_SLAB):
        rows = slice(r0, r0 + OUT_SLAB)
        mb = jnp.dot(yb_ref[rows, :], wpb_ref[...], preferred_element_type=F32)
        m = ma_ref[rows, :].astype(F32) + gb_ref[rows, :].astype(F32) * mb
        r = jnp.dot(m.astype(BF16), wo_ref[...], preferred_element_type=F32)
        inv = lax.rsqrt(jnp.mean(r * r, axis=-1, keepdims=True) + EPS)
        o_ref[rows, :] = x_ref[rows, :] + r * inv * gpost_ref[...]


def _out_call(layer, yb, gb, ma, x, wpb, wo, gpost):
    m = x.shape[0]
    tm = TM_OUT
    row = pl.BlockSpec((tm, D_MODEL), lambda i: (i, 0))
    return pl.pallas_call(
        _out_kernel,
        grid=(m // tm,),
        in_specs=[row, row, row, row,
                  _layer_spec((D_B, D_MODEL), layer), _layer_spec((D_MODEL, D_MODEL), layer),
                  _const_spec((1, D_MODEL))],
        out_specs=row,
        out_shape=jax.ShapeDtypeStruct((m, D_MODEL), F32),
        compiler_params=pltpu.CompilerParams(
            dimension_semantics=("arbitrary",), vmem_limit_bytes=48 * MIB),
        name="merge_out",
    )(yb, gb, ma, x, wpb, wo, gpost)


def kernel(x, w_in, b_gate, g_pre, g_post, sgu_ln_g, sgu_ln_b, w_spatial, b_spatial,
           w_proj_a, w_proj_b, w_out):
    batch, seq, d_model = x.shape
    depth = w_in.shape[0]
    assert d_model == D_MODEL and w_in.shape[2] == N_IN
    assert seq % (16 * 256) == 0 and (batch * seq) % TM_OUT == 0
    w_in, w_spatial, w_proj_a, w_proj_b, w_out = (
        w.astype(BF16) for w in (w_in, w_spatial, w_proj_a, w_proj_b, w_out))
    xf = x.reshape(batch * seq, d_model)
    for l in range(depth):
        q, k, v, zb, gb, ma = _proj_call(
            l, xf, g_pre[l][None], w_in, b_gate[l][None],
            sgu_ln_g[l][None], sgu_ln_b[l][None], w_spatial, b_spatial[l].T, w_proj_a)
        yb = _attn_call(q, k, v, zb, batch, seq)
        xf = _out_call(l, yb, gb, ma, xf, w_proj_b, w_out, g_post[l][None])
    return xf.reshape(batch, seq, d_model)
```

```python
import math

import jax
import jax.numpy as jnp
from jax import lax
from jax.experimental import pallas as pl
from jax.experimental.pallas import tpu as pltpu

D_MODEL = 1024
D_A = 2 * D_MODEL
CHUNK = 128
G_A = D_A // 128
N_HEADS = 8
HEAD_DIM = 128
D_B = N_HEADS * HEAD_DIM
DILATIONS = (1, 4, 16)
N_SIDE = 64
NEG = -1e30
EPS = 1e-6
LOG2E = math.log2(math.e)

OFF_U = 0
OFF_V = OFF_U + D_A
OFF_ZA = OFF_V + D_A
OFF_Q = OFF_ZA + D_A
OFF_K = OFF_Q + D_B
OFF_VB = OFF_K + D_B
OFF_ZB = OFF_VB + D_B
OFF_GA = OFF_ZB + D_B
OFF_GB = OFF_GA + D_MODEL
N_IN = OFF_GB + D_MODEL

QBLK = 128
KBLK = QBLK + 2 * N_SIDE
GROUP = 4
MIX_ROWS = 32
LAYOUT_ITERS = 8
TM_PROJ = 256
TM_OUT = 1024
OUT_SLAB = 256
CW = 512
MIB = 1024 * 1024

F32 = jnp.float32
BF16 = jnp.bfloat16


def _gelu(x):
    c = math.sqrt(2.0 / math.pi)
    return 0.5 * x * (1.0 + jnp.tanh(c * (x + 0.044715 * (x * x * x))))


def _sigmoid(x):
    return 0.5 * (jnp.tanh(0.5 * x) + 1.0)


def _silu(x):
    return x * _sigmoid(x)


def _const_spec(shape):
    nd = len(shape)
    return pl.BlockSpec(shape, lambda *_: (0,) * nd, pipeline_mode=pl.Buffered(1))


def _layer_spec(shape, layer):
    nd = len(shape)
    return pl.BlockSpec((None,) + tuple(shape), lambda *_: (layer,) + (0,) * nd,
                        pipeline_mode=pl.Buffered(1))


def _proj_kernel(x_ref, gpre_ref, w_in_ref, bgate_ref, lng_ref, lnb_ref, ws_ref, bst_ref, wpa_ref,
                 q_ref, k_ref, v_ref, zb_ref, gb_ref, ma_ref,
                 t_scr, v_scr, vn_scr, ya_scr):
    tm = x_ref.shape[0]
    x = x_ref[...]
    inv = lax.rsqrt(jnp.mean(x * x, axis=-1, keepdims=True) + EPS)
    h = (x * inv * gpre_ref[...]).astype(BF16)

    def proj(off, c):
        return jnp.dot(h, w_in_ref[:, off + c:off + c + CW], preferred_element_type=F32)

    for c in range(0, D_A, CW):
        v_scr[:, c:c + CW] = _gelu(proj(OFF_V, c))

    v = v_scr[...]
    mu = jnp.mean(v, axis=-1, keepdims=True)
    dv = v - mu
    var = jnp.mean(dv * dv, axis=-1, keepdims=True)
    vn_scr[...] = (dv * lax.rsqrt(var + EPS) * lng_ref[...] + lnb_ref[...]).astype(BF16)

    scale = LOG2E * HEAD_DIM ** -0.5

    def per_head(out_ref, val, c):
        for hd in range(c // HEAD_DIM, (c + CW) // HEAD_DIM):
            lo = hd * HEAD_DIM - c
            out_ref[hd] = val[:, lo:lo + HEAD_DIM]

    def attn_branch(c):
        cs = slice(c, c + CW)
        gb_ref[:, cs] = _sigmoid(proj(OFF_GB, c) + bgate_ref[:, D_MODEL + c:D_MODEL + c + CW]).astype(BF16)
        per_head(zb_ref, _silu(proj(OFF_ZB, c)).astype(BF16), c)
        per_head(q_ref, proj(OFF_Q, c) * scale, c)
        per_head(k_ref, proj(OFF_K, c), c)
        per_head(v_ref, proj(OFF_VB, c), c)

    for c in range(0, D_A, CW):
        t_scr[:, c:c + CW] = _gelu(proj(OFF_U, c)) * _silu(proj(OFF_ZA, c))
        if c % (2 * CW) == 0:
            attn_branch(c // 2)

    assert tm % (2 * CHUNK) == 0
    for c in range(0, tm, 2 * CHUNK):
        rows0 = slice(c, c + CHUNK)
        rows1 = slice(c + CHUNK, c + 2 * CHUNK)
        for g in range(G_A):
            cols = slice(g * 128, (g + 1) * 128)
            vn2 = jnp.concatenate([vn_scr[rows0, cols], vn_scr[rows1, cols]], axis=1)
            s2 = jnp.dot(ws_ref[g], vn2, preferred_element_type=F32) + bst_ref[:, g:g + 1]
            ya_scr[rows0, cols] = (t_scr[rows0, cols] * s2[:, :128]).astype(BF16)
            ya_scr[rows1, cols] = (t_scr[rows1, cols] * s2[:, 128:]).astype(BF16)

    for c in range(0, D_MODEL, CW):
        cs = slice(c, c + CW)
        ya_p = jnp.dot(ya_scr[...], wpa_ref[:, cs], preferred_element_type=F32)
        ma_ref[:, cs] = (_sigmoid(proj(OFF_GA, c) + bgate_ref[:, cs]) * ya_p).astype(BF16)


def _proj_call(layer, x, gpre, w_in, bgate, lng, lnb, ws, bst, wpa):
    m = x.shape[0]
    tm = TM_PROJ
    row = lambda i: (i, 0)
    heads = pl.BlockSpec((N_HEADS, tm, HEAD_DIM), lambda i: (0, i, 0))
    heads_f32 = jax.ShapeDtypeStruct((N_HEADS, m, HEAD_DIM), F32)
    heads_bf16 = jax.ShapeDtypeStruct((N_HEADS, m, HEAD_DIM), BF16)
    out_bf16 = jax.ShapeDtypeStruct((m, D_B), BF16)
    return pl.pallas_call(
        _proj_kernel,
        grid=(m // tm,),
        in_specs=[
            pl.BlockSpec((tm, D_MODEL), row),
            _const_spec((1, D_MODEL)),
            _layer_spec((D_MODEL, N_IN), layer),
            _const_spec((1, 2 * D_MODEL)),
            _const_spec((1, D_A)),
            _const_spec((1, D_A)),
            _layer_spec((G_A, CHUNK, CHUNK), layer),
            _const_spec((CHUNK, G_A)),
            _layer_spec((D_A, D_MODEL), layer),
        ],
        out_specs=[heads] * 4 + [pl.BlockSpec((tm, D_B), row)] * 2,
        out_shape=[heads_f32, heads_f32, heads_f32, heads_bf16, out_bf16, out_bf16],
        scratch_shapes=[
            pltpu.VMEM((tm, D_A), F32),
            pltpu.VMEM((tm, D_A), F32),
            pltpu.VMEM((tm, D_A), BF16),
            pltpu.VMEM((tm, D_A), BF16),
        ],
        compiler_params=pltpu.CompilerParams(
            dimension_semantics=("arbitrary",), vmem_limit_bytes=56 * MIB),
        name="proj_gmlp",
    )(x, gpre, w_in, bgate, lng, lnb, ws, bst, wpa)


def _attn_kernel(q_ref, k_ref, v_ref, zb_ref, y_ref, qd, kd, vd, f4q, f4k, f4v, o_scr, d_scr, m_scr,
                 s_scr, p_scr, bias_scr):
    seq = q_ref.shape[0]
    n_pat = len(DILATIONS)
    blocks_per_pat = seq // QBLK
    head = pl.program_id(1)

    slope = LOG2E * jnp.exp2(jnp.full((QBLK, KBLK), -1.0, F32) * (head + 1).astype(F32))
    qi = lax.broadcasted_iota(jnp.int32, (QBLK, KBLK), 0)
    ki = lax.broadcasted_iota(jnp.int32, (QBLK, KBLK), 1)
    for p, d in enumerate(DILATIONS):
        for case, shift in enumerate((N_SIDE, 0, KBLK - QBLK)):
            rel = jnp.abs(ki - shift - qi)
            bias_scr[3 * p + case] = jnp.where(rel <= N_SIDE, -slope * (d * rel).astype(F32), NEG)

    assert DILATIONS == (1, 4, 16)
    rows_per_copy = 256
    n_copies = seq // rows_per_copy
    srcs = (q_ref, k_ref, v_ref)
    dsts = (qd, kd, vd)
    mids = (f4q, f4k, f4v)

    def cast_natural(i):
        dst = slice(i * rows_per_copy, (i + 1) * rows_per_copy)
        for src, out in zip(srcs, dsts):
            out[0, dst, :] = src[dst, :].astype(BF16)

    def deinterleave4(i):
        dst = slice(i * rows_per_copy, (i + 1) * rows_per_copy)
        src4 = pl.ds(i // 4 + 4 * rows_per_copy * (i % 4), rows_per_copy, stride=4)
        for src, mid, out in zip(srcs, mids, dsts):
            x4 = src[src4, :]
            mid[dst, :] = x4
            out[1, dst, :] = x4.astype(BF16)

    def deinterleave16(i):
        dst = slice(i * rows_per_copy, (i + 1) * rows_per_copy)
        src = pl.ds((i % 4) * (seq // 4) + i // 4, rows_per_copy, stride=4)
        for mid, out in zip(mids, dsts):
            out[2, dst, :] = mid[src, :].astype(BF16)

    assert seq // 16 == rows_per_copy
    for i in range(n_copies):
        cast_natural(i)
    layout_work = [(deinterleave4, i) for i in range(n_copies)] + [(deinterleave16, i) for i in range(n_copies)]

    schedule = [(p, j) for p in (0, 2, 1) for j in range(blocks_per_pat)]
    assert sorted(schedule) == [(p, j) for p in range(n_pat) for j in range(blocks_per_pat)]
    od_slab = {0: 0, 2: 1}
    y_nat = f4q

    def block_addr(g):
        p, j = schedule[g]
        d = DILATIONS[p]
        sub_len = seq // d
        per_res = blocks_per_pat // d
        r, n = divmod(j, per_res)
        case = 1 if n == 0 else (2 if n == per_res - 1 else 0)
        q0 = j * QBLK
        k0 = r * sub_len + min(max(QBLK * n - N_SIDE, 0), sub_len - KBLK)
        if d == 16:
            out_rows = pl.ds((r % 4) * (seq // 4) + r // 4 + 4 * QBLK * n, QBLK, stride=4)
        else:
            out_rows = pl.ds(q0, QBLK)
        nat_rows = r + 4 * QBLK * n if d == 4 else None
        return p, q0, k0, 3 * p + case, out_rows, nat_rows

    def scores_stage(grp, slot):
        for b in range(GROUP):
            p, q0, k0, bias_idx, _, _ = block_addr(grp * GROUP + b)
            s_scr[slot, b * QBLK:(b + 1) * QBLK, :] = lax.dot_general(
                qd[p, q0:q0 + QBLK, :], kd[p, k0:k0 + KBLK, :],
                (((1,), (1,)), ((), ())), preferred_element_type=F32) + bias_scr[bias_idx]

    def rowmax_stage(grp, slot):
        for b in range(GROUP):
            p, _, _, _, out_rows, _ = block_addr(grp * GROUP + b)
            mx = jnp.max(s_scr[slot, b * QBLK:(b + 1) * QBLK, :], axis=-1, keepdims=True)
            m_scr[p, out_rows, :] = jnp.broadcast_to(mx, (QBLK, HEAD_DIM))

    def exp_stage(grp, s_slot, p_slot):
        for b in range(GROUP):
            p, _, _, _, out_rows, _ = block_addr(grp * GROUP + b)
            rows = slice(b * QBLK, (b + 1) * QBLK)
            mx = m_scr[p, out_rows, :]
            p_scr[p_slot, rows, :] = jnp.exp2(
                s_scr[s_slot, rows, :] - jnp.concatenate([mx, mx], axis=1)).astype(BF16)

    ones = jnp.ones((KBLK, HEAD_DIM), BF16)

    def values_stage(grp, slot):
        for b in range(GROUP):
            p, q0, k0, _, out_rows, nat0 = block_addr(grp * GROUP + b)
            v_aug = jnp.concatenate([vd[p, k0:k0 + KBLK, :], ones], axis=1)
            oa = jnp.dot(p_scr[slot, b * QBLK:(b + 1) * QBLK, :], v_aug, preferred_element_type=F32)
            if nat0 is None:
                o_scr[od_slab[p], out_rows, :] = oa[:, :HEAD_DIM]
                d_scr[od_slab[p], out_rows, :] = oa[:, HEAD_DIM:]
                continue
            for c in range(0, QBLK, MIX_ROWS):
                by4 = pl.ds(q0 + c, MIX_ROWS)
                nat = pl.ds(nat0 + 4 * c, MIX_ROWS, stride=4)
                a1, a4, a16 = m_scr[0, nat, :], m_scr[1, by4, :], m_scr[2, by4, :]
                mx = jnp.maximum(jnp.maximum(a1, a4), a16)
                w1, w4, w16 = jnp.exp2(a1 - mx), jnp.exp2(a4 - mx), jnp.exp2(a16 - mx)
                o4, den4 = oa[c:c + MIX_ROWS, :HEAD_DIM], oa[c:c + MIX_ROWS, HEAD_DIM:]
                num = w1 * o_scr[0, nat, :] + w4 * o4 + w16 * o_scr[1, by4, :]
                den = w1 * d_scr[0, nat, :] + w4 * den4 + w16 * d_scr[1, by4, :]
                y_nat[nat, :] = num / den

    assert KBLK == 2 * HEAD_DIM
    n_groups = n_pat * blocks_per_pat // GROUP
    per_iter = -(-len(layout_work) // LAYOUT_ITERS)
    for t in range(n_groups + 3):
        for fn, i in layout_work[t * per_iter:(t + 1) * per_iter]:
            fn(i)
        if 0 <= t - 3:
            values_stage(t - 3, (t - 3) % 2)
        if 0 <= t - 2 < n_groups:
            exp_stage(t - 2, (t - 2) % 3, (t - 2) % 2)
        if 0 <= t - 1 < n_groups:
            rowmax_stage(t - 1, (t - 1) % 3)
        if t < n_groups:
            scores_stage(t, t % 3)

    y_ref[...] = (y_nat[...] * zb_ref[...].astype(F32)).astype(BF16)


def _attn_call(q, k, v, zb, batch, seq):
    n_pat = len(DILATIONS)
    blk = pl.BlockSpec((seq, HEAD_DIM), lambda b, h: (b, h))
    slab = pl.BlockSpec((None, seq, HEAD_DIM), lambda b, h: (h, b, 0))
    pat_f32 = pltpu.VMEM((n_pat, seq, HEAD_DIM), F32)
    pat_bf16 = pltpu.VMEM((n_pat, seq, HEAD_DIM), BF16)
    return pl.pallas_call(
        _attn_kernel,
        grid=(batch, N_HEADS),
        in_specs=[slab, slab, slab, slab],
        out_specs=blk,
        out_shape=jax.ShapeDtypeStruct((batch * seq, D_B), BF16),
        scratch_shapes=[pat_bf16] * 3 + [pltpu.VMEM((seq, HEAD_DIM), F32)] * 3
        + [pltpu.VMEM((2, seq, HEAD_DIM), F32)] * 2 + [pat_f32]
        + [pltpu.VMEM((3, GROUP * QBLK, KBLK), F32),
           pltpu.VMEM((2, GROUP * QBLK, KBLK), BF16),
           pltpu.VMEM((3 * n_pat, QBLK, KBLK), F32)],
        compiler_params=pltpu.CompilerParams(
            dimension_semantics=("arbitrary", "arbitrary"), vmem_limit_bytes=58 * MIB),
        name="dilated_attn",
    )(q, k, v, zb)


def _out_kernel(yb_ref, gb_ref, ma_ref, x_ref, wpb_ref, wo_ref, gpost_ref, o_ref):
    tm = o_ref.shape[0]
    for r0 in range(0, tm, OUT_SLAB):
        rows = slice(r0, r0 + OUT_SLAB)
        mb = jnp.dot(yb_ref[rows, :], wpb_ref[...], preferred_element_type=F32)
        m = ma_ref[rows, :].astype(F32) + gb_ref[rows, :].astype(F32) * mb
        r = jnp.dot(m.astype(BF16), wo_ref[...], preferred_element_type=F32)
        inv = lax.rsqrt(jnp.mean(r * r, axis=-1, keepdims=True) + EPS)
        o_ref[rows, :] = x_ref[rows, :] + r * inv * gpost_ref[...]


def _out_call(layer, yb, gb, ma, x, wpb, wo, gpost):
    m = x.shape[0]
    tm = TM_OUT
    row = pl.BlockSpec((tm, D_MODEL), lambda i: (i, 0))
    return pl.pallas_call(
        _out_kernel,
        grid=(m // tm,),
        in_specs=[row, row, row, row,
                  _layer_spec((D_B, D_MODEL), layer), _layer_spec((D_MODEL, D_MODEL), layer),
                  _const_spec((1, D_MODEL))],
        out_specs=row,
        out_shape=jax.ShapeDtypeStruct((m, D_MODEL), F32),
        compiler_params=pltpu.CompilerParams(
            dimension_semantics=("arbitrary",), vmem_limit_bytes=48 * MIB),
        name="merge_out",
    )(yb, gb, ma, x, wpb, wo, gpost)


def kernel(x, w_in, b_gate, g_pre, g_post, sgu_ln_g, sgu_ln_b, w_spatial, b_spatial,
           w_proj_a, w_proj_b, w_out):
    batch, seq, d_model = x.shape
    depth = w_in.shape[0]
    assert d_model == D_MODEL and w_in.shape[2] == N_IN
    assert seq % (16 * 256) == 0 and (batch * seq) % TM_OUT == 0
    w_in, w_spatial, w_proj_a, w_proj_b, w_out = (
        w.astype(BF16) for w in (w_in, w_spatial, w_proj_a, w_proj_b, w_out))
    xf = x.reshape(batch * seq, d_model)
    for l in range(depth):
        q, k, v, zb, gb, ma = _proj_call(
            l, xf, g_pre[l][None], w_in, b_gate[l][None],
            sgu_ln_g[l][None], sgu_ln_b[l][None], w_spatial, b_spatial[l].T, w_proj_a)
        yb = _attn_call(q, k, v, zb, batch, seq)
        xf = _out_call(l, yb, gb, ma, xf, w_proj_b, w_out, g_post[l][None])
    return xf.reshape(batch, seq, d_model)
```

```python
import math

import jax
import jax.numpy as jnp
from jax import lax
from jax.experimental import pallas as pl
from jax.experimental.pallas import tpu as pltpu

D_MODEL = 1024
D_A = 2 * D_MODEL
CHUNK = 128
G_A = D_A // 128
N_HEADS = 8
HEAD_DIM = 128
D_B = N_HEADS * HEAD_DIM
DILATIONS = (1, 4, 16)
N_SIDE = 64
NEG = -1e30
EPS = 1e-6
LOG2E = math.log2(math.e)

OFF_U = 0
OFF_V = OFF_U + D_A
OFF_ZA = OFF_V + D_A
OFF_Q = OFF_ZA + D_A
OFF_K = OFF_Q + D_B
OFF_VB = OFF_K + D_B
OFF_ZB = OFF_VB + D_B
OFF_GA = OFF_ZB + D_B
OFF_GB = OFF_GA + D_MODEL
N_IN = OFF_GB + D_MODEL

QBLK = 128
KBLK = QBLK + 2 * N_SIDE
GROUP = 4
MIX_ROWS = 32
LAYOUT_ITERS = 8
TM_PROJ = 256
TM_OUT = 1024
OUT_SLAB = 256
CW = 512
MIB = 1024 * 1024

F32 = jnp.float32
BF16 = jnp.bfloat16


def _gelu(x):
    c = math.sqrt(2.0 / math.pi)
    return 0.5 * x * (1.0 + jnp.tanh(c * (x + 0.044715 * (x * x * x))))


def _sigmoid(x):
    return 0.5 * (jnp.tanh(0.5 * x) + 1.0)


def _silu(x):
    return x * _sigmoid(x)


def _const_spec(shape):
    nd = len(shape)
    return pl.BlockSpec(shape, lambda *_: (0,) * nd, pipeline_mode=pl.Buffered(1))


def _layer_spec(shape, layer):
    nd = len(shape)
    return pl.BlockSpec((None,) + tuple(shape), lambda *_: (layer,) + (0,) * nd,
                        pipeline_mode=pl.Buffered(1))


def _proj_kernel(x_ref, gpre_ref, w_in_ref, bgate_ref, lng_ref, lnb_ref, ws_ref, bst_ref, wpa_ref,
                 q_ref, k_ref, v_ref, zb_ref, gb_ref, ma_ref,
                 t_scr, v_scr, vn_scr, ya_scr):
    tm = x_ref.shape[0]
    x = x_ref[...]
    inv = lax.rsqrt(jnp.mean(x * x, axis=-1, keepdims=True) + EPS)
    h = (x * inv * gpre_ref[...]).astype(BF16)

    def proj(off, c):
        return jnp.dot(h, w_in_ref[:, off + c:off + c + CW], preferred_element_type=F32)

    for c in range(0, D_A, CW):
        v_scr[:, c:c + CW] = _gelu(proj(OFF_V, c))

    v = v_scr[...]
    mu = jnp.mean(v, axis=-1, keepdims=True)
    dv = v - mu
    var = jnp.mean(dv * dv, axis=-1, keepdims=True)
    vn_scr[...] = (dv * lax.rsqrt(var + EPS) * lng_ref[...] + lnb_ref[...]).astype(BF16)

    scale = LOG2E * HEAD_DIM ** -0.5

    def attn_branch(c):
        cs = slice(c, c + CW)
        gb_ref[:, cs] = _sigmoid(proj(OFF_GB, c) + bgate_ref[:, D_MODEL + c:D_MODEL + c + CW]).astype(BF16)
        zb_ref[:, cs] = _silu(proj(OFF_ZB, c)).astype(BF16)
        q_ref[:, cs] = proj(OFF_Q, c) * scale
        k_ref[:, cs] = proj(OFF_K, c)
        v_ref[:, cs] = proj(OFF_VB, c)

    for c in range(0, D_A, CW):
        t_scr[:, c:c + CW] = _gelu(proj(OFF_U, c)) * _silu(proj(OFF_ZA, c))
        if c % (2 * CW) == 0:
            attn_branch(c // 2)

    assert tm % (2 * CHUNK) == 0
    for c in range(0, tm, 2 * CHUNK):
        rows0 = slice(c, c + CHUNK)
        rows1 = slice(c + CHUNK, c + 2 * CHUNK)
        for g in range(G_A):
            cols = slice(g * 128, (g + 1) * 128)
            vn2 = jnp.concatenate([vn_scr[rows0, cols], vn_scr[rows1, cols]], axis=1)
            s2 = jnp.dot(ws_ref[g], vn2, preferred_element_type=F32) + bst_ref[:, g:g + 1]
            ya_scr[rows0, cols] = (t_scr[rows0, cols] * s2[:, :128]).astype(BF16)
            ya_scr[rows1, cols] = (t_scr[rows1, cols] * s2[:, 128:]).astype(BF16)

    for c in range(0, D_MODEL, CW):
        cs = slice(c, c + CW)
        ya_p = jnp.dot(ya_scr[...], wpa_ref[:, cs], preferred_element_type=F32)
        ma_ref[:, cs] = (_sigmoid(proj(OFF_GA, c) + bgate_ref[:, cs]) * ya_p).astype(BF16)


def _proj_call(layer, x, gpre, w_in, bgate, lng, lnb, ws, bst, wpa):
    m = x.shape[0]
    tm = TM_PROJ
    row = lambda i: (i, 0)
    out_f32 = jax.ShapeDtypeStruct((m, D_B), F32)
    out_bf16 = jax.ShapeDtypeStruct((m, D_B), BF16)
    return pl.pallas_call(
        _proj_kernel,
        grid=(m // tm,),
        in_specs=[
            pl.BlockSpec((tm, D_MODEL), row),
            _const_spec((1, D_MODEL)),
            _layer_spec((D_MODEL, N_IN), layer),
            _const_spec((1, 2 * D_MODEL)),
            _const_spec((1, D_A)),
            _const_spec((1, D_A)),
            _layer_spec((G_A, CHUNK, CHUNK), layer),
            _const_spec((CHUNK, G_A)),
            _layer_spec((D_A, D_MODEL), layer),
        ],
        out_specs=[pl.BlockSpec((tm, D_B), row)] * 6,
        out_shape=[out_f32, out_f32, out_f32, out_bf16, out_bf16, out_bf16],
        scratch_shapes=[
            pltpu.VMEM((tm, D_A), F32),
            pltpu.VMEM((tm, D_A), F32),
            pltpu.VMEM((tm, D_A), BF16),
            pltpu.VMEM((tm, D_A), BF16),
        ],
        compiler_params=pltpu.CompilerParams(
            dimension_semantics=("arbitrary",), vmem_limit_bytes=56 * MIB),
        name="proj_gmlp",
    )(x, gpre, w_in, bgate, lng, lnb, ws, bst, wpa)


def _attn_kernel(q_ref, k_ref, v_ref, zb_ref, y_ref, qd, kd, vd, f4q, f4k, f4v, o_scr, d_scr, m_scr,
                 s_scr, p_scr, bias_scr):
    seq = q_ref.shape[0]
    n_pat = len(DILATIONS)
    blocks_per_pat = seq // QBLK
    head = pl.program_id(1)

    slope = LOG2E * jnp.exp2(jnp.full((QBLK, KBLK), -1.0, F32) * (head + 1).astype(F32))
    qi = lax.broadcasted_iota(jnp.int32, (QBLK, KBLK), 0)
    ki = lax.broadcasted_iota(jnp.int32, (QBLK, KBLK), 1)
    for p, d in enumerate(DILATIONS):
        for case, shift in enumerate((N_SIDE, 0, KBLK - QBLK)):
            rel = jnp.abs(ki - shift - qi)
            bias_scr[3 * p + case] = jnp.where(rel <= N_SIDE, -slope * (d * rel).astype(F32), NEG)

    assert DILATIONS == (1, 4, 16)
    rows_per_copy = 256
    n_copies = seq // rows_per_copy
    srcs = (q_ref, k_ref, v_ref)
    dsts = (qd, kd, vd)
    mids = (f4q, f4k, f4v)

    def cast_natural(i):
        dst = slice(i * rows_per_copy, (i + 1) * rows_per_copy)
        for src, out in zip(srcs, dsts):
            out[0, dst, :] = src[dst, :].astype(BF16)

    def deinterleave4(i):
        dst = slice(i * rows_per_copy, (i + 1) * rows_per_copy)
        src4 = pl.ds(i // 4 + 4 * rows_per_copy * (i % 4), rows_per_copy, stride=4)
        for src, mid, out in zip(srcs, mids, dsts):
            x4 = src[src4, :]
            mid[dst, :] = x4
            out[1, dst, :] = x4.astype(BF16)

    def deinterleave16(i):
        dst = slice(i * rows_per_copy, (i + 1) * rows_per_copy)
        src = pl.ds((i % 4) * (seq // 4) + i // 4, rows_per_copy, stride=4)
        for mid, out in zip(mids, dsts):
            out[2, dst, :] = mid[src, :].astype(BF16)

    assert seq // 16 == rows_per_copy
    for i in range(n_copies):
        cast_natural(i)
    layout_work = [(deinterleave4, i) for i in range(n_copies)] + [(deinterleave16, i) for i in range(n_copies)]

    schedule = [(p, j) for p in (0, 2, 1) for j in range(blocks_per_pat)]
    assert sorted(schedule) == [(p, j) for p in range(n_pat) for j in range(blocks_per_pat)]
    od_slab = {0: 0, 2: 1}
    y_nat = f4q

    def block_addr(g):
        p, j = schedule[g]
        d = DILATIONS[p]
        sub_len = seq // d
        per_res = blocks_per_pat // d
        r, n = divmod(j, per_res)
        case = 1 if n == 0 else (2 if n == per_res - 1 else 0)
        q0 = j * QBLK
        k0 = r * sub_len + min(max(QBLK * n - N_SIDE, 0), sub_len - KBLK)
        if d == 16:
            out_rows = pl.ds((r % 4) * (seq // 4) + r // 4 + 4 * QBLK * n, QBLK, stride=4)
        else:
            out_rows = pl.ds(q0, QBLK)
        nat_rows = r + 4 * QBLK * n if d == 4 else None
        return p, q0, k0, 3 * p + case, out_rows, nat_rows

    def scores_stage(grp, slot):
        for b in range(GROUP):
            p, q0, k0, bias_idx, _, _ = block_addr(grp * GROUP + b)
            s_scr[slot, b * QBLK:(b + 1) * QBLK, :] = lax.dot_general(
                qd[p, q0:q0 + QBLK, :], kd[p, k0:k0 + KBLK, :],
                (((1,), (1,)), ((), ())), preferred_element_type=F32) + bias_scr[bias_idx]

    def rowmax_stage(grp, slot):
        for b in range(GROUP):
            p, _, _, _, out_rows, _ = block_addr(grp * GROUP + b)
            mx = jnp.max(s_scr[slot, b * QBLK:(b + 1) * QBLK, :], axis=-1, keepdims=True)
            m_scr[p, out_rows, :] = jnp.broadcast_to(mx, (QBLK, HEAD_DIM))

    def exp_stage(grp, s_slot, p_slot):
        for b in range(GROUP):
            p, _, _, _, out_rows, _ = block_addr(grp * GROUP + b)
            rows = slice(b * QBLK, (b + 1) * QBLK)
            mx = m_scr[p, out_rows, :]
            p_scr[p_slot, rows, :] = jnp.exp2(
                s_scr[s_slot, rows, :] - jnp.concatenate([mx, mx], axis=1)).astype(BF16)

    ones = jnp.ones((KBLK, HEAD_DIM), BF16)

    def values_stage(grp, slot):
        for b in range(GROUP):
            p, q0, k0, _, out_rows, nat0 = block_addr(grp * GROUP + b)
            v_aug = jnp.concatenate([vd[p, k0:k0 + KBLK, :], ones], axis=1)
            oa = jnp.dot(p_scr[slot, b * QBLK:(b + 1) * QBLK, :], v_aug, preferred_element_type=F32)
            if nat0 is None:
                o_scr[od_slab[p], out_rows, :] = oa[:, :HEAD_DIM]
                d_scr[od_slab[p], out_rows, :] = oa[:, HEAD_DIM:]
                continue
            for c in range(0, QBLK, MIX_ROWS):
                by4 = pl.ds(q0 + c, MIX_ROWS)
                nat = pl.ds(nat0 + 4 * c, MIX_ROWS, stride=4)
                a1, a4, a16 = m_scr[0, nat, :], m_scr[1, by4, :], m_scr[2, by4, :]
                mx = jnp.maximum(jnp.maximum(a1, a4), a16)
                w1, w4, w16 = jnp.exp2(a1 - mx), jnp.exp2(a4 - mx), jnp.exp2(a16 - mx)
                o4, den4 = oa[c:c + MIX_ROWS, :HEAD_DIM], oa[c:c + MIX_ROWS, HEAD_DIM:]
                num = w1 * o_scr[0, nat, :] + w4 * o4 + w16 * o_scr[1, by4, :]
                den = w1 * d_scr[0, nat, :] + w4 * den4 + w16 * d_scr[1, by4, :]
                y_nat[nat, :] = num / den

    assert KBLK == 2 * HEAD_DIM
    n_groups = n_pat * blocks_per_pat // GROUP
    per_iter = -(-len(layout_work) // LAYOUT_ITERS)
    for t in range(n_groups + 3):
        for fn, i in layout_work[t * per_iter:(t + 1) * per_iter]:
            fn(i)
        if 0 <= t - 3:
            values_stage(t - 3, (t - 3) % 2)
        if 0 <= t - 2 < n_groups:
            exp_stage(t - 2, (t - 2) % 3, (t - 2) % 2)
        if 0 <= t - 1 < n_groups:
            rowmax_stage(t - 1, (t - 1) % 3)
        if t < n_groups:
            scores_stage(t, t % 3)

    y_ref[...] = (y_nat[...] * zb_ref[...].astype(F32)).astype(BF16)


def _attn_call(q, k, v, zb, batch, seq):
    n_pat = len(DILATIONS)
    blk = pl.BlockSpec((seq, HEAD_DIM), lambda b, h: (b, h))
    pat_f32 = pltpu.VMEM((n_pat, seq, HEAD_DIM), F32)
    pat_bf16 = pltpu.VMEM((n_pat, seq, HEAD_DIM), BF16)
    return pl.pallas_call(
        _attn_kernel,
        grid=(batch, N_HEADS),
        in_specs=[blk, blk, blk, blk],
        out_specs=blk,
        out_shape=jax.ShapeDtypeStruct((batch * seq, D_B), BF16),
        scratch_shapes=[pat_bf16] * 3 + [pltpu.VMEM((seq, HEAD_DIM), F32)] * 3
        + [pltpu.VMEM((2, seq, HEAD_DIM), F32)] * 2 + [pat_f32]
        + [pltpu.VMEM((3, GROUP * QBLK, KBLK), F32),
           pltpu.VMEM((2, GROUP * QBLK, KBLK), BF16),
           pltpu.VMEM((3 * n_pat, QBLK, KBLK), F32)],
        compiler_params=pltpu.CompilerParams(
            dimension_semantics=("arbitrary", "arbitrary"), vmem_limit_bytes=58 * MIB),
        name="dilated_attn",
    )(q, k, v, zb)


def _out_kernel(yb_ref, gb_ref, ma_ref, x_ref, wpb_ref, wo_ref, gpost_ref, o_ref):
    tm = o_ref.shape[0]
    for r0 in range(0, tm, OUT_SLAB):
        rows = slice(r0, r0 + OUT_SLAB)
        mb = jnp.dot(yb_ref[rows, :], wpb_ref[...], preferred_element_type=F32)
        m = ma_ref[rows, :].astype(F32) + gb_ref[rows, :].astype(F32) * mb
        r = jnp.dot(m.astype(BF16), wo_ref[...], preferred_element_type=F32)
        inv = lax.rsqrt(jnp.mean(r * r, axis=-1, keepdims=True) + EPS)
        o_ref[rows, :] = x_ref[rows, :] + r * inv * gpost_ref[...]


def _out_call(layer, yb, gb, ma, x, wpb, wo, gpost):
    m = x.shape[0]
    tm = TM_OUT
    row = pl.BlockSpec((tm, D_MODEL), lambda i: (i, 0))
    return pl.pallas_call(
        _out_kernel,
        grid=(m // tm,),
        in_specs=[row, row, row, row,
                  _layer_spec((D_B, D_MODEL), layer), _layer_spec((D_MODEL, D_MODEL), layer),
                  _const_spec((1, D_MODEL))],
        out_specs=row,
        out_shape=jax.ShapeDtypeStruct((m, D_MODEL), F32),
        compiler_params=pltpu.CompilerParams(
            dimension_semantics=("arbitrary",), vmem_limit_bytes=48 * MIB),
        name="merge_out",
    )(yb, gb, ma, x, wpb, wo, gpost)


def kernel(x, w_in, b_gate, g_pre, g_post, sgu_ln_g, sgu_ln_b, w_spatial, b_spatial,
           w_proj_a, w_proj_b, w_out):
    batch, seq, d_model = x.shape
    depth = w_in.shape[0]
    assert d_model == D_MODEL and w_in.shape[2] == N_IN
    assert seq % (16 * 256) == 0 and (batch * seq) % TM_OUT == 0
    w_in, w_spatial, w_proj_a, w_proj_b, w_out = (
        w.astype(BF16) for w in (w_in, w_spatial, w_proj_a, w_proj_b, w_out))
    xf = x.reshape(batch * seq, d_model)
    for l in range(depth):
        q, k, v, zb, gb, ma = _proj_call(
            l, xf, g_pre[l][None], w_in, b_gate[l][None],
            sgu_ln_g[l][None], sgu_ln_b[l][None], w_spatial, b_spatial[l].T, w_proj_a)
        yb = _attn_call(q, k, v, zb, batch, seq)
        xf = _out_call(l, yb, gb, ma, xf, w_proj_b, w_out, g_post[l][None])
    return xf.reshape(batch, seq, d_model)
```

```python
import math

import jax
import jax.numpy as jnp
from jax import lax
from jax.experimental import pallas as pl
from jax.experimental.pallas import tpu as pltpu

D_MODEL = 1024
D_A = 2 * D_MODEL
CHUNK = 128
G_A = D_A // 128
N_HEADS = 8
HEAD_DIM = 128
D_B = N_HEADS * HEAD_DIM
DILATIONS = (1, 4, 16)
N_SIDE = 64
NEG = -1e30
EPS = 1e-6
LOG2E = math.log2(math.e)

OFF_U = 0
OFF_V = OFF_U + D_A
OFF_ZA = OFF_V + D_A
OFF_Q = OFF_ZA + D_A
OFF_K = OFF_Q + D_B
OFF_VB = OFF_K + D_B
OFF_ZB = OFF_VB + D_B
OFF_GA = OFF_ZB + D_B
OFF_GB = OFF_GA + D_MODEL
N_IN = OFF_GB + D_MODEL

QBLK = 128
KBLK = QBLK + 2 * N_SIDE
GROUP = 4
MIX_ROWS = 32
LAYOUT_ITERS = 8
TM_PROJ = 256
TM_OUT = 1024
OUT_SLAB = 256
CW = 512
MIB = 1024 * 1024

F32 = jnp.float32
BF16 = jnp.bfloat16


def _gelu(x):
    c = math.sqrt(2.0 / math.pi)
    return 0.5 * x * (1.0 + jnp.tanh(c * (x + 0.044715 * (x * x * x))))


def _sigmoid(x):
    return 0.5 * (jnp.tanh(0.5 * x) + 1.0)


def _silu(x):
    return x * _sigmoid(x)


def _const_spec(shape):
    nd = len(shape)
    return pl.BlockSpec(shape, lambda *_: (0,) * nd, pipeline_mode=pl.Buffered(1))


def _layer_spec(shape, layer):
    nd = len(shape)
    return pl.BlockSpec((None,) + tuple(shape), lambda *_: (layer,) + (0,) * nd,
                        pipeline_mode=pl.Buffered(1))


def _proj_kernel(x_ref, gpre_ref, w_in_ref, bgate_ref, lng_ref, lnb_ref, ws_ref, bst_ref, wpa_ref,
                 q_ref, k_ref, v_ref, zb_ref, gb_ref, ma_ref,
                 t_scr, v_scr, vn_scr, ya_scr):
    tm = x_ref.shape[0]
    x = x_ref[...]
    inv = lax.rsqrt(jnp.mean(x * x, axis=-1, keepdims=True) + EPS)
    h = (x * gpre_ref[...]).astype(BF16)

    def proj(off, c):
        return jnp.dot(h, w_in_ref[:, off + c:off + c + CW], preferred_element_type=F32) * inv

    for c in range(0, D_A, CW):
        v_scr[:, c:c + CW] = _gelu(proj(OFF_V, c))

    v = v_scr[...]
    mu = jnp.mean(v, axis=-1, keepdims=True)
    dv = v - mu
    var = jnp.mean(dv * dv, axis=-1, keepdims=True)
    vn_scr[...] = (dv * lax.rsqrt(var + EPS) * lng_ref[...] + lnb_ref[...]).astype(BF16)

    scale = LOG2E * HEAD_DIM ** -0.5

    def attn_branch(c):
        cs = slice(c, c + CW)
        gb_ref[:, cs] = _sigmoid(proj(OFF_GB, c) + bgate_ref[:, D_MODEL + c:D_MODEL + c + CW]).astype(BF16)
        zb_ref[:, cs] = _silu(proj(OFF_ZB, c)).astype(BF16)
        q_ref[:, cs] = proj(OFF_Q, c) * scale
        k_ref[:, cs] = proj(OFF_K, c)
        v_ref[:, cs] = proj(OFF_VB, c)

    for c in range(0, D_A, CW):
        t_scr[:, c:c + CW] = _gelu(proj(OFF_U, c)) * _silu(proj(OFF_ZA, c))
        if c % (2 * CW) == 0:
            attn_branch(c // 2)

    assert tm % (2 * CHUNK) == 0
    for c in range(0, tm, 2 * CHUNK):
        rows0 = slice(c, c + CHUNK)
        rows1 = slice(c + CHUNK, c + 2 * CHUNK)
        for g in range(G_A):
            cols = slice(g * 128, (g + 1) * 128)
            vn2 = jnp.concatenate([vn_scr[rows0, cols], vn_scr[rows1, cols]], axis=1)
            s2 = jnp.dot(ws_ref[g], vn2, preferred_element_type=F32) + bst_ref[:, g:g + 1]
            ya_scr[rows0, cols] = (t_scr[rows0, cols] * s2[:, :128]).astype(BF16)
            ya_scr[rows1, cols] = (t_scr[rows1, cols] * s2[:, 128:]).astype(BF16)

    for c in range(0, D_MODEL, CW):
        cs = slice(c, c + CW)
        ya_p = jnp.dot(ya_scr[...], wpa_ref[:, cs], preferred_element_type=F32)
        ma_ref[:, cs] = (_sigmoid(proj(OFF_GA, c) + bgate_ref[:, cs]) * ya_p).astype(BF16)


def _proj_call(layer, x, gpre, w_in, bgate, lng, lnb, ws, bst, wpa):
    m = x.shape[0]
    tm = TM_PROJ
    row = lambda i: (i, 0)
    out_f32 = jax.ShapeDtypeStruct((m, D_B), F32)
    out_bf16 = jax.ShapeDtypeStruct((m, D_B), BF16)
    return pl.pallas_call(
        _proj_kernel,
        grid=(m // tm,),
        in_specs=[
            pl.BlockSpec((tm, D_MODEL), row),
            _const_spec((1, D_MODEL)),
            _layer_spec((D_MODEL, N_IN), layer),
            _const_spec((1, 2 * D_MODEL)),
            _const_spec((1, D_A)),
            _const_spec((1, D_A)),
            _layer_spec((G_A, CHUNK, CHUNK), layer),
            _const_spec((CHUNK, G_A)),
            _layer_spec((D_A, D_MODEL), layer),
        ],
        out_specs=[pl.BlockSpec((tm, D_B), row)] * 6,
        out_shape=[out_f32, out_f32, out_f32, out_bf16, out_bf16, out_bf16],
        scratch_shapes=[
            pltpu.VMEM((tm, D_A), F32),
            pltpu.VMEM((tm, D_A), F32),
            pltpu.VMEM((tm, D_A), BF16),
            pltpu.VMEM((tm, D_A), BF16),
        ],
        compiler_params=pltpu.CompilerParams(
            dimension_semantics=("arbitrary",), vmem_limit_bytes=56 * MIB),
        name="proj_gmlp",
    )(x, gpre, w_in, bgate, lng, lnb, ws, bst, wpa)


def _attn_kernel(q_ref, k_ref, v_ref, zb_ref, y_ref, qd, kd, vd, f4q, f4k, f4v, o_scr, d_scr, m_scr,
                 s_scr, p_scr, bias_scr):
    seq = q_ref.shape[0]
    n_pat = len(DILATIONS)
    blocks_per_pat = seq // QBLK
    head = pl.program_id(1)

    slope = LOG2E * jnp.exp2(jnp.full((QBLK, KBLK), -1.0, F32) * (head + 1).astype(F32))
    qi = lax.broadcasted_iota(jnp.int32, (QBLK, KBLK), 0)
    ki = lax.broadcasted_iota(jnp.int32, (QBLK, KBLK), 1)
    for p, d in enumerate(DILATIONS):
        for case, shift in enumerate((N_SIDE, 0, KBLK - QBLK)):
            rel = jnp.abs(ki - shift - qi)
            bias_scr[3 * p + case] = jnp.where(rel <= N_SIDE, -slope * (d * rel).astype(F32), NEG)

    assert DILATIONS == (1, 4, 16)
    rows_per_copy = 256
    n_copies = seq // rows_per_copy
    srcs = (q_ref, k_ref, v_ref)
    dsts = (qd, kd, vd)
    mids = (f4q, f4k, f4v)

    def cast_natural(i):
        dst = slice(i * rows_per_copy, (i + 1) * rows_per_copy)
        for src, out in zip(srcs, dsts):
            out[0, dst, :] = src[dst, :].astype(BF16)

    def deinterleave4(i):
        dst = slice(i * rows_per_copy, (i + 1) * rows_per_copy)
        src4 = pl.ds(i // 4 + 4 * rows_per_copy * (i % 4), rows_per_copy, stride=4)
        for src, mid, out in zip(srcs, mids, dsts):
            x4 = src[src4, :]
            mid[dst, :] = x4
            out[1, dst, :] = x4.astype(BF16)

    def deinterleave16(i):
        dst = slice(i * rows_per_copy, (i + 1) * rows_per_copy)
        src = pl.ds((i % 4) * (seq // 4) + i // 4, rows_per_copy, stride=4)
        for mid, out in zip(mids, dsts):
            out[2, dst, :] = mid[src, :].astype(BF16)

    assert seq // 16 == rows_per_copy
    for i in range(n_copies):
        cast_natural(i)
    layout_work = [(deinterleave4, i) for i in range(n_copies)] + [(deinterleave16, i) for i in range(n_copies)]

    schedule = [(p, j) for p in (0, 2, 1) for j in range(blocks_per_pat)]
    assert sorted(schedule) == [(p, j) for p in range(n_pat) for j in range(blocks_per_pat)]
    od_slab = {0: 0, 2: 1}
    y_nat = f4q

    def block_addr(g):
        p, j = schedule[g]
        d = DILATIONS[p]
        sub_len = seq // d
        per_res = blocks_per_pat // d
        r, n = divmod(j, per_res)
        case = 1 if n == 0 else (2 if n == per_res - 1 else 0)
        q0 = j * QBLK
        k0 = r * sub_len + min(max(QBLK * n - N_SIDE, 0), sub_len - KBLK)
        if d == 16:
            out_rows = pl.ds((r % 4) * (seq // 4) + r // 4 + 4 * QBLK * n, QBLK, stride=4)
        else:
            out_rows = pl.ds(q0, QBLK)
        nat_rows = r + 4 * QBLK * n if d == 4 else None
        return p, q0, k0, 3 * p + case, out_rows, nat_rows

    def scores_stage(grp, slot):
        for b in range(GROUP):
            p, q0, k0, bias_idx, _, _ = block_addr(grp * GROUP + b)
            s_scr[slot, b * QBLK:(b + 1) * QBLK, :] = lax.dot_general(
                qd[p, q0:q0 + QBLK, :], kd[p, k0:k0 + KBLK, :],
                (((1,), (1,)), ((), ())), preferred_element_type=F32) + bias_scr[bias_idx]

    def rowmax_stage(grp, slot):
        for b in range(GROUP):
            p, _, _, _, out_rows, _ = block_addr(grp * GROUP + b)
            mx = jnp.max(s_scr[slot, b * QBLK:(b + 1) * QBLK, :], axis=-1, keepdims=True)
            m_scr[p, out_rows, :] = jnp.broadcast_to(mx, (QBLK, HEAD_DIM))

    def exp_stage(grp, s_slot, p_slot):
        for b in range(GROUP):
            p, _, _, _, out_rows, _ = block_addr(grp * GROUP + b)
            rows = slice(b * QBLK, (b + 1) * QBLK)
            mx = m_scr[p, out_rows, :]
            p_scr[p_slot, rows, :] = jnp.exp2(
                s_scr[s_slot, rows, :] - jnp.concatenate([mx, mx], axis=1)).astype(BF16)

    ones = jnp.ones((KBLK, HEAD_DIM), BF16)

    def values_stage(grp, slot):
        for b in range(GROUP):
            p, q0, k0, _, out_rows, nat0 = block_addr(grp * GROUP + b)
            v_aug = jnp.concatenate([vd[p, k0:k0 + KBLK, :], ones], axis=1)
            oa = jnp.dot(p_scr[slot, b * QBLK:(b + 1) * QBLK, :], v_aug, preferred_element_type=F32)
            if nat0 is None:
                o_scr[od_slab[p], out_rows, :] = oa[:, :HEAD_DIM]
                d_scr[od_slab[p], out_rows, :] = oa[:, HEAD_DIM:]
                continue
            for c in range(0, QBLK, MIX_ROWS):
                by4 = pl.ds(q0 + c, MIX_ROWS)
                nat = pl.ds(nat0 + 4 * c, MIX_ROWS, stride=4)
                a1, a4, a16 = m_scr[0, nat, :], m_scr[1, by4, :], m_scr[2, by4, :]
                mx = jnp.maximum(jnp.maximum(a1, a4), a16)
                w1, w4, w16 = jnp.exp2(a1 - mx), jnp.exp2(a4 - mx), jnp.exp2(a16 - mx)
                o4, den4 = oa[c:c + MIX_ROWS, :HEAD_DIM], oa[c:c + MIX_ROWS, HEAD_DIM:]
                num = w1 * o_scr[0, nat, :] + w4 * o4 + w16 * o_scr[1, by4, :]
                den = w1 * d_scr[0, nat, :] + w4 * den4 + w16 * d_scr[1, by4, :]
                y_nat[nat, :] = num / den

    assert KBLK == 2 * HEAD_DIM
    n_groups = n_pat * blocks_per_pat // GROUP
    per_iter = -(-len(layout_work) // LAYOUT_ITERS)
    for t in range(n_groups + 3):
        for fn, i in layout_work[t * per_iter:(t + 1) * per_iter]:
            fn(i)
        if 0 <= t - 3:
            values_stage(t - 3, (t - 3) % 2)
        if 0 <= t - 2 < n_groups:
            exp_stage(t - 2, (t - 2) % 3, (t - 2) % 2)
        if 0 <= t - 1 < n_groups:
            rowmax_stage(t - 1, (t - 1) % 3)
        if t < n_groups:
            scores_stage(t, t % 3)

    y_ref[...] = (y_nat[...] * zb_ref[...].astype(F32)).astype(BF16)


def _attn_call(q, k, v, zb, batch, seq):
    n_pat = len(DILATIONS)
    blk = pl.BlockSpec((seq, HEAD_DIM), lambda b, h: (b, h))
    pat_f32 = pltpu.VMEM((n_pat, seq, HEAD_DIM), F32)
    pat_bf16 = pltpu.VMEM((n_pat, seq, HEAD_DIM), BF16)
    return pl.pallas_call(
        _attn_kernel,
        grid=(batch, N_HEADS),
        in_specs=[blk, blk, blk, blk],
        out_specs=blk,
        out_shape=jax.ShapeDtypeStruct((batch * seq, D_B), BF16),
        scratch_shapes=[pat_bf16] * 3 + [pltpu.VMEM((seq, HEAD_DIM), F32)] * 3
        + [pltpu.VMEM((2, seq, HEAD_DIM), F32)] * 2 + [pat_f32]
        + [pltpu.VMEM((3, GROUP * QBLK, KBLK), F32),
           pltpu.VMEM((2, GROUP * QBLK, KBLK), BF16),
           pltpu.VMEM((3 * n_pat, QBLK, KBLK), F32)],
        compiler_params=pltpu.CompilerParams(
            dimension_semantics=("arbitrary", "arbitrary"), vmem_limit_bytes=58 * MIB),
        name="dilated_attn",
    )(q, k, v, zb)


def _out_kernel(yb_ref, gb_ref, ma_ref, x_ref, wpb_ref, wo_ref, gpost_ref, o_ref):
    tm = o_ref.shape[0]
    for r0 in range(0, tm, OUT_SLAB):
        rows = slice(r0, r0 + OUT_SLAB)
        mb = jnp.dot(yb_ref[rows, :], wpb_ref[...], preferred_element_type=F32)
        m = ma_ref[rows, :].astype(F32) + gb_ref[rows, :].astype(F32) * mb
        r = jnp.dot(m.astype(BF16), wo_ref[...], preferred_element_type=F32)
        inv = lax.rsqrt(jnp.mean(r * r, axis=-1, keepdims=True) + EPS)
        o_ref[rows, :] = x_ref[rows, :] + r * inv * gpost_ref[...]


def _out_call(layer, yb, gb, ma, x, wpb, wo, gpost):
    m = x.shape[0]
    tm = TM_OUT
    row = pl.BlockSpec((tm, D_MODEL), lambda i: (i, 0))
    return pl.pallas_call(
        _out_kernel,
        grid=(m // tm,),
        in_specs=[row, row, row, row,
                  _layer_spec((D_B, D_MODEL), layer), _layer_spec((D_MODEL, D_MODEL), layer),
                  _const_spec((1, D_MODEL))],
        out_specs=row,
        out_shape=jax.ShapeDtypeStruct((m, D_MODEL), F32),
        compiler_params=pltpu.CompilerParams(
            dimension_semantics=("arbitrary",), vmem_limit_bytes=48 * MIB),
        name="merge_out",
    )(yb, gb, ma, x, wpb, wo, gpost)


def kernel(x, w_in, b_gate, g_pre, g_post, sgu_ln_g, sgu_ln_b, w_spatial, b_spatial,
           w_proj_a, w_proj_b, w_out):
    batch, seq, d_model = x.shape
    depth = w_in.shape[0]
    assert d_model == D_MODEL and w_in.shape[2] == N_IN
    assert seq % (16 * 256) == 0 and (batch * seq) % TM_OUT == 0
    w_in, w_spatial, w_proj_a, w_proj_b, w_out = (
        w.astype(BF16) for w in (w_in, w_spatial, w_proj_a, w_proj_b, w_out))
    xf = x.reshape(batch * seq, d_model)
    for l in range(depth):
        q, k, v, zb, gb, ma = _proj_call(
            l, xf, g_pre[l][None], w_in, b_gate[l][None],
            sgu_ln_g[l][None], sgu_ln_b[l][None], w_spatial, b_spatial[l].T, w_proj_a)
        yb = _attn_call(q, k, v, zb, batch, seq)
        xf = _out_call(l, yb, gb, ma, xf, w_proj_b, w_out, g_post[l][None])
    return xf.reshape(batch, seq, d_model)
```
